```python
import math
import jax
import jax.numpy as jnp
from jax import lax
import numpy as np


D_MODEL = 1024
BATCH = 4
SEQ = 8192
DEPTH = 2
DEC_BATCH = 32
DEC_SEQ = 4
PAST_LEN = 16384
PAGE_SIZE = 128

N_A_LAYERS = DEPTH // 2
N_B_LAYERS = DEPTH - N_A_LAYERS
CONV_EXPAND = 2
CONV_CH = CONV_EXPAND * D_MODEL
CONV_WIDTH = 31
HEAD_DIM = 64
HEADS_PER_GROUP = 8
N_GROUPS = 3
WINDOWS = (128, 512, 2048)
DILATIONS = (1, 4, 16)
ATTN_Q_WIDTH = N_GROUPS * HEADS_PER_GROUP * HEAD_DIM
ATTN_OUT_WIDTH = HEADS_PER_GROUP * HEAD_DIM
ROT_DIM = HEAD_DIM // 4
ROPE_THETA = 500000.0
NORM_EPS = 1e-6
BLOCK_Q = 128
ATTN_SCALE = HEAD_DIM ** -0.5
NEG_INF = -1e30

kernel_name = 'conv_yoco_dilated_swa_step'


def _rmsnorm(x, g):
    xf = x.astype(jnp.float32)
    y = xf * lax.rsqrt(jnp.mean(xf * xf, axis=-1, keepdims=True) + NORM_EPS)
    return (y * g.astype(jnp.float32)).astype(x.dtype)


def _layernorm(x, g, b):
    xf = x.astype(jnp.float32)
    xc = xf - jnp.mean(xf, axis=-1, keepdims=True)
    var = jnp.mean(xc * xc, axis=-1, keepdims=True)
    y = xc * lax.rsqrt(var + NORM_EPS) * g.astype(jnp.float32) + b.astype(jnp.float32)
    return y.astype(x.dtype)


def _rope(x, pos):
    half = ROT_DIM // 2
    inv_freq = jnp.exp(-math.log(ROPE_THETA) * jnp.arange(half, dtype=jnp.float32) * (2.0 / ROT_DIM))
    ang = pos.astype(jnp.float32)[:, None] * inv_freq[None, :]
    ang = ang.reshape((pos.shape[0],) + (1,) * (x.ndim - 3) + (half,))
    cos, sin = jnp.cos(ang), jnp.sin(ang)
    xf = x.astype(jnp.float32)
    x1, x2 = xf[..., :half], xf[..., half:ROT_DIM]
    out = jnp.concatenate([x1 * cos - x2 * sin, x2 * cos + x1 * sin, xf[..., ROT_DIM:]], axis=-1)
    return out.astype(x.dtype)


def _conv_layer(x, hist, g_norm, w_in, conv_w, conv_b, ln_g, ln_b, w_out):
    h = _rmsnorm(x, g_norm)
    val, glu_gate, z = jnp.split(h @ w_in, 3, axis=-1)
    v = val * jax.nn.sigmoid(glu_gate)
    full = jnp.concatenate([hist.astype(v.dtype), v], axis=1)
    c = lax.conv_general_dilated(full, conv_w[:, None, :].astype(full.dtype), window_strides=(1,),
                                 padding='VALID', dimension_numbers=('NWC', 'WIO', 'NWC'),
                                 feature_group_count=CONV_CH) + conv_b.astype(full.dtype)
    c = jax.nn.silu(_layernorm(c, ln_g, ln_b))
    y = (c * jax.nn.silu(z)) @ w_out
    return x + y.astype(x.dtype), full[:, -(CONV_WIDTH - 1):]


def _shared_kv(x, pos, kv_norm, w_kv, k_gain):
    bx, length = x.shape[:2]
    h = _rmsnorm(x, kv_norm)
    kv = (h @ w_kv).reshape(bx, length, 2, N_GROUPS, HEADS_PER_GROUP, HEAD_DIM)
    k = _rope(_rmsnorm(kv[:, :, 0], k_gain), pos)
    return k, kv[:, :, 1]


def _query_side(x, pos, g_norm, w_in, q_gain):
    bx, length = x.shape[:2]
    u = _rmsnorm(x, g_norm) @ w_in
    q = u[..., :ATTN_Q_WIDTH].reshape(bx, length, N_GROUPS, HEADS_PER_GROUP, HEAD_DIM)
    q = _rope(_rmsnorm(q, q_gain), pos)
    return q, u[..., ATTN_Q_WIDTH:]


def _band_attention(q, k, v, span):
    assert span <= BLOCK_Q
    lead = q.shape[:-2]
    m_len = q.shape[-2]
    nb = -(-m_len // BLOCK_Q)
    extra = nb * BLOCK_Q - m_len
    pad_q = [(0, 0)] * len(lead) + [(0, extra), (0, 0)]
    pad_kv = [(0, 0)] * len(lead) + [(BLOCK_Q, extra), (0, 0)]
    qb = jnp.pad(q, pad_q).reshape(lead + (nb, BLOCK_Q, HEAD_DIM))
    kb = jnp.pad(k, pad_kv).reshape(lead + (nb + 1, BLOCK_Q, HEAD_DIM))
    vb = jnp.pad(v, pad_kv).reshape(lead + (nb + 1, BLOCK_Q, HEAD_DIM))
    k_band = jnp.concatenate([kb[..., :-1, :, :], kb[..., 1:, :, :]], axis=-2)
    v_band = jnp.concatenate([vb[..., :-1, :, :], vb[..., 1:, :, :]], axis=-2)
    s = jnp.einsum('...nqd,...nkd->...nqk', qb, k_band, preferred_element_type=jnp.float32) * ATTN_SCALE
    qi = jnp.arange(BLOCK_Q)[:, None]
    kj = jnp.arange(2 * BLOCK_Q)[None, :]
    dist = qi + BLOCK_Q - kj
    kpos = jnp.arange(nb)[:, None, None] * BLOCK_Q - BLOCK_Q + kj[None]
    valid = (dist >= 0)[None] & (dist <= span)[None] & (kpos >= 0)
    s = jnp.where(valid, s, NEG_INF)
    mx = jnp.max(s, axis=-1, keepdims=True)
    e = jnp.exp(s - mx)
    den = jnp.sum(e, axis=-1, keepdims=True)
    o = jnp.einsum('...nqk,...nkd->...nqd', e, v_band.astype(jnp.float32)) / den
    lse = (mx + jnp.log(den))[..., 0]
    o = o.reshape(lead + (nb * BLOCK_Q, HEAD_DIM))[..., :m_len, :]
    lse = lse.reshape(lead + (nb * BLOCK_Q,))[..., :m_len]
    return o, lse


def _dilated_group_prompt(q, k, v, window, dilation):
    bx, s_len = q.shape[:2]
    m_len = s_len // dilation

    def to_sub(t):
        return t.reshape(bx, m_len, dilation, HEADS_PER_GROUP, HEAD_DIM).transpose(0, 2, 3, 1, 4)

    o, lse = _band_attention(to_sub(q), to_sub(k), to_sub(v), window // dilation)
    o = o.transpose(0, 3, 1, 2, 4).reshape(bx, s_len, HEADS_PER_GROUP, HEAD_DIM)
    lse = lse.transpose(0, 3, 1, 2).reshape(bx, s_len, HEADS_PER_GROUP)
    return o, lse


def _dilated_group_sample(q, k_cat, v_cat, buf_len, window, dilation):
    t_len = q.shape[1]
    span = window // dilation
    idx = buf_len + jnp.arange(t_len)[:, None] - dilation * jnp.arange(span + 1)[None, :]
    valid = idx >= 0
    idx = jnp.maximum(idx, 0)
    kg = jnp.take(k_cat, idx, axis=1)
    vg = jnp.take(v_cat, idx, axis=1)
    s = jnp.einsum('bthd,btmhd->bhtm', q, kg, preferred_element_type=jnp.float32) * ATTN_SCALE
    s = jnp.where(valid[None, None], s, NEG_INF)
    mx = jnp.max(s, axis=-1, keepdims=True)
    e = jnp.exp(s - mx)
    den = jnp.sum(e, axis=-1, keepdims=True)
    o = jnp.einsum('bhtm,btmhd->bthd', e, vg.astype(jnp.float32)) / den[..., 0].transpose(0, 2, 1)[..., None]
    lse = (mx + jnp.log(den))[..., 0].transpose(0, 2, 1)
    return o, lse


def _merge_groups(outs, lses):
    w = jax.nn.softmax(jnp.stack(lses, axis=0), axis=0)
    return jnp.sum(w[..., None] * jnp.stack(outs, axis=0), axis=0)


def _attn_out(o, z, w_out):
    bx, length = o.shape[:2]
    return (o.reshape(bx, length, ATTN_OUT_WIDTH).astype(z.dtype) * jax.nn.silu(z)) @ w_out


def _window_state_sample(cache, k_new, v_new):
    cat = jnp.concatenate([cache, jnp.stack([k_new, v_new], axis=2).astype(cache.dtype)], axis=1)
    return cat[:, cat.shape[1] - cache.shape[1]:]


def _window_state_prompt(k, v, window):
    rows = min(window, k.shape[1])
    return jnp.stack([k[:, k.shape[1] - rows:], v[:, v.shape[1] - rows:]], axis=2)


def setup_inputs(seed: int = 0) -> dict:
    key = jax.random.key(seed)
    ks = jax.random.split(key, 24)

    def nrm(k, shape, scale):
        return scale * jax.random.normal(k, shape, jnp.float32)

    return {
        'x_prompt': nrm(ks[0], (BATCH, SEQ, D_MODEL), 1.0),
        'x_sample': nrm(ks[1], (DEC_BATCH, DEC_SEQ, D_MODEL), 1.0),
        'cache_conv': nrm(ks[2], (N_A_LAYERS, DEC_BATCH, CONV_WIDTH - 1, CONV_CH), 0.5),
        'cache_kv_w128': nrm(ks[3], (DEC_BATCH, min(WINDOWS[0], PAST_LEN), 2, HEADS_PER_GROUP, HEAD_DIM), 1.0),
        'cache_kv_w512': nrm(ks[4], (DEC_BATCH, min(WINDOWS[1], PAST_LEN), 2, HEADS_PER_GROUP, HEAD_DIM), 1.0),
        'cache_kv_w2048': nrm(ks[5], (DEC_BATCH, min(WINDOWS[2], PAST_LEN), 2, HEADS_PER_GROUP, HEAD_DIM), 1.0),
        'a_norm': 1.0 + nrm(ks[6], (N_A_LAYERS, D_MODEL), 0.02),
        'a_w_in': nrm(ks[7], (N_A_LAYERS, D_MODEL, 3 * CONV_CH), D_MODEL ** -0.5),
        'a_conv_w': nrm(ks[8], (N_A_LAYERS, CONV_WIDTH, CONV_CH), CONV_WIDTH ** -0.5),
        'a_conv_b': nrm(ks[9], (N_A_LAYERS, CONV_CH), 0.01),
        'a_ln_g': 1.0 + nrm(ks[10], (N_A_LAYERS, CONV_CH), 0.02),
        'a_ln_b': nrm(ks[11], (N_A_LAYERS, CONV_CH), 0.01),
        'a_w_out': nrm(ks[12], (N_A_LAYERS, CONV_CH, D_MODEL), CONV_CH ** -0.5),
        'kv_norm': 1.0 + nrm(ks[13], (D_MODEL,), 0.02),
        'w_kv': nrm(ks[14], (D_MODEL, 2 * ATTN_Q_WIDTH), D_MODEL ** -0.5),
        'k_norm': 1.0 + nrm(ks[15], (HEAD_DIM,), 0.02),
        'b_norm': 1.0 + nrm(ks[16], (N_B_LAYERS, D_MODEL), 0.02),
        'b_w_in': nrm(ks[17], (N_B_LAYERS, D_MODEL, ATTN_Q_WIDTH + ATTN_OUT_WIDTH), D_MODEL ** -0.5),
        'q_norm': 1.0 + nrm(ks[18], (N_B_LAYERS, HEAD_DIM), 0.02),
        'b_w_out': nrm(ks[19], (N_B_LAYERS, ATTN_OUT_WIDTH, D_MODEL), ATTN_OUT_WIDTH ** -0.5),
    }


def reference(x_prompt, x_sample, cache_conv, cache_kv_w128, cache_kv_w512, cache_kv_w2048,
              a_norm, a_w_in, a_conv_w, a_conv_b, a_ln_g, a_ln_b, a_w_out,
              kv_norm, w_kv, k_norm, b_norm, b_w_in, q_norm, b_w_out):
    s_len = x_prompt.shape[1]
    t_len = x_sample.shape[1]
    pos_p = jnp.arange(s_len, dtype=jnp.int32)
    pos_s = PAST_LEN + jnp.arange(t_len, dtype=jnp.int32)
    caches = (cache_kv_w128, cache_kv_w512, cache_kv_w2048)
    xp, xs = x_prompt, x_sample
    conv_p, conv_s = [], []
    kp = vp = ks = vs = None
    k_cat, v_cat = [], []
    for layer in range(DEPTH):
        if layer < N_A_LAYERS:
            a = layer
            hist0 = jnp.zeros((xp.shape[0], CONV_WIDTH - 1, CONV_CH), xp.dtype)
            xp, hp = _conv_layer(xp, hist0, a_norm[a], a_w_in[a], a_conv_w[a], a_conv_b[a],
                                 a_ln_g[a], a_ln_b[a], a_w_out[a])
            xs, hs = _conv_layer(xs, cache_conv[a], a_norm[a], a_w_in[a], a_conv_w[a], a_conv_b[a],
                                 a_ln_g[a], a_ln_b[a], a_w_out[a])
            conv_p.append(hp)
            conv_s.append(hs)
        else:
            if layer == N_A_LAYERS:
                kp, vp = _shared_kv(xp, pos_p, kv_norm, w_kv, k_norm)
                ks, vs = _shared_kv(xs, pos_s, kv_norm, w_kv, k_norm)
                for g in range(N_GROUPS):
                    k_cat.append(jnp.concatenate([caches[g][:, :, 0].astype(ks.dtype), ks[:, :, g]], axis=1))
                    v_cat.append(jnp.concatenate([caches[g][:, :, 1].astype(vs.dtype), vs[:, :, g]], axis=1))
            b = layer - N_A_LAYERS
            qp, zp = _query_side(xp, pos_p, b_norm[b], b_w_in[b], q_norm[b])
            qs, zs = _query_side(xs, pos_s, b_norm[b], b_w_in[b], q_norm[b])
            outs_p, lses_p, outs_s, lses_s = [], [], [], []
            for g in range(N_GROUPS):
                o, l = _dilated_group_prompt(qp[:, :, g], kp[:, :, g], vp[:, :, g], WINDOWS[g], DILATIONS[g])
                outs_p.append(o)
                lses_p.append(l)
                o, l = _dilated_group_sample(qs[:, :, g], k_cat[g], v_cat[g], caches[g].shape[1],
                                             WINDOWS[g], DILATIONS[g])
                outs_s.append(o)
                lses_s.append(l)
            xp = xp + _attn_out(_merge_groups(outs_p, lses_p), zp, b_w_out[b]).astype(xp.dtype)
            xs = xs + _attn_out(_merge_groups(outs_s, lses_s), zs, b_w_out[b]).astype(xs.dtype)
    new_conv_prompt = jnp.stack(conv_p, axis=0)
    new_conv_sample = jnp.stack(conv_s, axis=0)
    new_kv_w128_prompt = _window_state_prompt(kp[:, :, 0], vp[:, :, 0], WINDOWS[0])
    new_kv_w128_sample = _window_state_sample(cache_kv_w128, ks[:, :, 0], vs[:, :, 0])
    new_kv_w512_prompt = _window_state_prompt(kp[:, :, 1], vp[:, :, 1], WINDOWS[1])
    new_kv_w512_sample = _window_state_sample(cache_kv_w512, ks[:, :, 1], vs[:, :, 1])
    new_kv_w2048_prompt = _window_state_prompt(kp[:, :, 2], vp[:, :, 2], WINDOWS[2])
    new_kv_w2048_sample = _window_state_sample(cache_kv_w2048, ks[:, :, 2], vs[:, :, 2])
    return (xp, xs, new_conv_prompt, new_conv_sample, new_kv_w128_prompt, new_kv_w128_sample,
            new_kv_w512_prompt, new_kv_w512_sample, new_kv_w2048_prompt, new_kv_w2048_sample)
```

```python
import functools
import math

import jax
import jax.numpy as jnp
from jax import lax
from jax.experimental import pallas as pl
from jax.experimental.pallas import tpu as pltpu

F32 = jnp.float32
BF16 = jnp.bfloat16

D_MODEL = 1024
CONV_CH = 2048
CONV_WIDTH = 31
HEAD_DIM = 64
HEADS = 8
GROUP_W = HEADS * HEAD_DIM
N_GROUPS = 3
Q_W = N_GROUPS * GROUP_W
WINDOWS = (128, 512, 2048)
DILATIONS = (1, 4, 16)
SPAN = 128
ROT_DIM = 16
ROPE_THETA = 500000.0
EPS = 1e-6
ATTN_SCALE = HEAD_DIM ** -0.5
NEG_INF = -1e30
PAST_LEN = 16384

LANES = 128
SUBLANES = 8
CH_GROUPS = CONV_CH // LANES
HALO = 32
VMEM_LIMIT = 56 * 1024 * 1024


def _sigmoid(x):
    return 1.0 / (1.0 + jnp.exp(-x))


def _const_spec(shape):
    return pl.BlockSpec(shape, lambda *_: (0,) * len(shape), pipeline_mode=pl.Buffered(1))


def _params(*sem):
    return pltpu.CompilerParams(dimension_semantics=sem, vmem_limit_bytes=VMEM_LIMIT)


def _inproj_kernel(x_ref, g_ref, w_ref, vmaj_ref, zs_ref, nat_ref, *, tm):
    x = x_ref[...]
    ms = jnp.mean(x * x, axis=-1, keepdims=True)
    h = ((x * lax.rsqrt(ms + EPS)) * g_ref[...]).astype(BF16)
    ch = 2 * LANES
    for j in range(CONV_CH // ch):
        val = jnp.dot(h, w_ref[:, j * ch:(j + 1) * ch], preferred_element_type=F32)
        gate = jnp.dot(h, w_ref[:, CONV_CH + j * ch:CONV_CH + (j + 1) * ch], preferred_element_type=F32)
        z = jnp.dot(h, w_ref[:, 2 * CONV_CH + j * ch:2 * CONV_CH + (j + 1) * ch], preferred_element_type=F32)
        v = val * _sigmoid(gate)
        zs_ref[:, j * ch:(j + 1) * ch] = (z * _sigmoid(z)).astype(BF16)
        for tb in range(tm // SUBLANES):
            for gg in range(ch // LANES):
                r = (tb * CH_GROUPS + j * (ch // LANES) + gg) * SUBLANES
                nat_ref[r:r + SUBLANES, :] = v[tb * SUBLANES:(tb + 1) * SUBLANES, gg * LANES:(gg + 1) * LANES]

    def to_major(tb, carry):
        base = tb * (CH_GROUPS * SUBLANES)
        for i in range(SUBLANES):
            lo = nat_ref[pl.ds(base + i, SUBLANES, stride=SUBLANES), :]
            hi = nat_ref[pl.ds(base + SUBLANES * SUBLANES + i, SUBLANES, stride=SUBLANES), :]
            vmaj_ref[tb * SUBLANES + i] = jnp.concatenate([lo, hi], axis=0).astype(BF16)
        return carry

    lax.fori_loop(0, tm // SUBLANES, to_major, 0)


def _inproj(x2d, gain, w_bf16, tm):
    rows = x2d.shape[0]
    return pl.pallas_call(
        functools.partial(_inproj_kernel, tm=tm),
        grid=(rows // tm,),
        in_specs=[pl.BlockSpec((tm, D_MODEL), lambda i: (i, 0)),
                  _const_spec((1, D_MODEL)),
                  _const_spec((D_MODEL, 3 * CONV_CH))],
        out_specs=[pl.BlockSpec((tm, CH_GROUPS, LANES), lambda i: (i, 0, 0)),
                   pl.BlockSpec((tm, CONV_CH), lambda i: (i, 0))],
        out_shape=[jax.ShapeDtypeStruct((rows, CH_GROUPS, LANES), BF16),
                   jax.ShapeDtypeStruct((rows, CONV_CH), BF16)],
        scratch_shapes=[pltpu.VMEM((tm * CH_GROUPS, LANES), F32)],
        compiler_params=_params("parallel"),
        name="inproj_glu",
    )(x2d, gain, w_bf16)


def _conv_taps(get_row, w_ref):
    accs = [None] * 4
    rows = {}
    for k in range(CONV_WIDTH):
        wk = w_ref[k].astype(F32)
        for u in range(4):
            j = u + k
            if j not in rows:
                rows[j] = get_row(j).astype(F32)
            term = wk * rows[j]
            accs[u] = term if accs[u] is None else accs[u] + term
    return accs


def _store_conv_rows(nat_ref, tb, i0, accs):
    base = tb * (CH_GROUPS * SUBLANES)
    for u, acc in enumerate(accs):
        nat_ref[pl.ds(base + i0 + u, SUBLANES, stride=SUBLANES), :] = acc[0:SUBLANES]
        nat_ref[pl.ds(base + SUBLANES * SUBLANES + i0 + u, SUBLANES, stride=SUBLANES), :] = acc[SUBLANES:]


def _ln_gate_rows(nat_ref, zs_ref, u_ref, cb_ref, lg_ref, lb_ref, nblk):
    def body(t2, carry):
        halves = []
        for half in range(2):
            base = pl.multiple_of((2 * t2 + half) * (CH_GROUPS * SUBLANES), CH_GROUPS * SUBLANES)
            halves.append(jnp.concatenate(
                [nat_ref[pl.ds(base + g * SUBLANES, SUBLANES), :] for g in range(CH_GROUPS)], axis=1))
        c = jnp.concatenate(halves, axis=0) + cb_ref[...]
        mean = jnp.mean(c, axis=-1, keepdims=True)
        xc = c - mean
        var = jnp.mean(xc * xc, axis=-1, keepdims=True)
        y = xc * lax.rsqrt(var + EPS) * lg_ref[...] + lb_ref[...]
        y = y * _sigmoid(y)
        r0 = pl.multiple_of(t2 * 16, 16)
        u_ref[pl.ds(r0, 16), :] = (y * zs_ref[pl.ds(r0, 16), :].astype(F32)).astype(BF16)
        return carry

    lax.fori_loop(0, nblk, body, 0)


def _conv_kernel(vmaj_ref, vhalo_ref, zs_ref, x_ref, w_ref, cb_ref, lg_ref, lb_ref, wout_ref, o_ref,
                 nat_ref, u_ref, *, t):
    first = pl.program_id(1) == 0

    def halo_row(r):
        row = vhalo_ref[r]
        return jnp.where(first, jnp.zeros_like(row), row)

    for t0 in range(0, HALO, 4):
        def get_row(j, t0=t0):
            r = t0 + 2 + j
            return halo_row(r) if r < HALO else vmaj_ref[r - HALO]
        accs = _conv_taps(get_row, w_ref)
        _store_conv_rows(nat_ref, t0 // SUBLANES, t0 % SUBLANES, accs)

    def main(tb, carry):
        for i0 in (0, 4):
            first_tok = tb * SUBLANES + i0
            accs = _conv_taps(lambda j: vmaj_ref[first_tok + 2 + j - HALO], w_ref)
            _store_conv_rows(nat_ref, tb, i0, accs)
        return carry

    lax.fori_loop(HALO // SUBLANES, t // SUBLANES, main, 0)

    _ln_gate_rows(nat_ref, zs_ref, u_ref, cb_ref, lg_ref, lb_ref, t // 16)
    o_ref[...] = x_ref[...] + jnp.dot(u_ref[...], wout_ref[...], preferred_element_type=F32)


def _conv_block(vmaj, zs, x, w_pack, cb, lg, lb, wout, t):
    b, s = x.shape[0], x.shape[1]
    hb = t // HALO
    return pl.pallas_call(
        functools.partial(_conv_kernel, t=t),
        grid=(b, s // t),
        in_specs=[pl.BlockSpec((None, t, CH_GROUPS, LANES), lambda bi, i: (bi, i, 0, 0)),
                  pl.BlockSpec((None, HALO, CH_GROUPS, LANES), lambda bi, i: (bi, jnp.maximum(i * hb - 1, 0), 0, 0)),
                  pl.BlockSpec((None, t, CONV_CH), lambda bi, i: (bi, i, 0)),
                  pl.BlockSpec((None, t, D_MODEL), lambda bi, i: (bi, i, 0)),
                  _const_spec((CONV_WIDTH, CH_GROUPS, LANES)),
                  _const_spec((1, CONV_CH)), _const_spec((1, CONV_CH)), _const_spec((1, CONV_CH)),
                  _const_spec((CONV_CH, D_MODEL))],
        out_specs=pl.BlockSpec((None, t, D_MODEL), lambda bi, i: (bi, i, 0)),
        out_shape=jax.ShapeDtypeStruct((b, s, D_MODEL), F32),
        scratch_shapes=[pltpu.VMEM((t * CH_GROUPS, LANES), F32), pltpu.VMEM((t, CONV_CH), BF16)],
        compiler_params=_params("parallel", "parallel"),
        name="conv_block",
    )(vmaj, vmaj, zs, x, w_pack, cb, lg, lb, wout)


def _conv_sample_kernel(full_ref, zs_ref, x_ref, w_ref, cb_ref, lg_ref, lb_ref, wout_ref, o_ref, *, tlen):
    lc = 4 * LANES
    for t in range(tlen):
        parts = []
        for c in range(CONV_CH // lc):
            acc = None
            for k in range(CONV_WIDTH):
                term = w_ref[k:k + 1, c * lc:(c + 1) * lc] * full_ref[t + k, :, c * lc:(c + 1) * lc]
                acc = term if acc is None else acc + term
            parts.append(acc)
        c_row = jnp.concatenate(parts, axis=1) + cb_ref[...]
        mean = jnp.mean(c_row, axis=-1, keepdims=True)
        xc = c_row - mean
        var = jnp.mean(xc * xc, axis=-1, keepdims=True)
        y = xc * lax.rsqrt(var + EPS) * lg_ref[...] + lb_ref[...]
        y = y * _sigmoid(y)
        u = (y * zs_ref[t].astype(F32)).astype(BF16)
        o_ref[t] = x_ref[t] + jnp.dot(u, wout_ref[...], preferred_element_type=F32)


def _conv_sample(full_t, zs_t, x_t, w, cb, lg, lb, wout):
    tlen, db = x_t.shape[0], x_t.shape[1]
    return pl.pallas_call(
        functools.partial(_conv_sample_kernel, tlen=tlen),
        out_shape=jax.ShapeDtypeStruct((tlen, db, D_MODEL), F32),
        compiler_params=pltpu.CompilerParams(vmem_limit_bytes=VMEM_LIMIT),
        name="conv_sample",
    )(full_t, zs_t, x_t, w, cb, lg, lb, wout)


def _qkv_kernel(x_ref, gkv_ref, gq_ref, wkv_ref, wq_ref, ebd_ref, kgain_ref, qgain_ref, cos_ref, sa_ref, sb_ref,
                q_ref, k_ref, v_ref, zs_ref):
    x = x_ref[...]
    xn = x * lax.rsqrt(jnp.mean(x * x, axis=-1, keepdims=True) + EPS)
    hk = (xn * gkv_ref[...]).astype(BF16)
    hq = (xn * gq_ref[...]).astype(BF16)
    cos, sa, sb = cos_ref[...], sa_ref[...], sb_ref[...]
    ebd = ebd_ref[...]
    cw = 2 * LANES

    def norm_rope(tile, gain):
        ss = jnp.dot((tile * tile).astype(BF16), ebd, preferred_element_type=F32)
        tn = tile * lax.rsqrt(ss * (1.0 / HEAD_DIM) + EPS) * gain
        outs = []
        for hh in range(cw // LANES):
            a = tn[:, hh * LANES:(hh + 1) * LANES]
            outs.append(a * cos + pltpu.roll(a, LANES - ROT_DIM // 2, 1) * sa + pltpu.roll(a, ROT_DIM // 2, 1) * sb)
        return jnp.concatenate(outs, axis=1)

    for c in range(Q_W // cw):
        sl = slice(c * cw, (c + 1) * cw)
        kc = jnp.dot(hk, wkv_ref[:, sl], preferred_element_type=F32)
        k_ref[:, sl] = norm_rope(kc, kgain_ref[:, sl]).astype(k_ref.dtype)
        vc = jnp.dot(hk, wkv_ref[:, Q_W + c * cw:Q_W + (c + 1) * cw], preferred_element_type=F32)
        v_ref[:, sl] = vc.astype(v_ref.dtype)
        qc = jnp.dot(hq, wq_ref[:, sl], preferred_element_type=F32)
        q_ref[:, sl] = (norm_rope(qc, qgain_ref[:, sl]) * ATTN_SCALE).astype(q_ref.dtype)
    for c in range(GROUP_W // cw):
        z = jnp.dot(hq, wq_ref[:, Q_W + c * cw:Q_W + (c + 1) * cw], preferred_element_type=F32)
        zs_ref[:, c * cw:(c + 1) * cw] = (z * _sigmoid(z)).astype(zs_ref.dtype)


def _qkv(x2d, gkv, gq, wkv, wq, ebd, kgain, qgain, tabs, tab_blocks, tm, out_dtype):
    rows = x2d.shape[0]
    tab_spec = pl.BlockSpec((tm, LANES), lambda i: (i % tab_blocks, 0))
    row_spec = lambda w: pl.BlockSpec((tm, w), lambda i: (i, 0))
    return pl.pallas_call(
        _qkv_kernel,
        grid=(rows // tm,),
        in_specs=[row_spec(D_MODEL), _const_spec((1, D_MODEL)), _const_spec((1, D_MODEL)),
                  _const_spec((D_MODEL, 2 * Q_W)), _const_spec((D_MODEL, Q_W + GROUP_W)),
                  _const_spec((2 * LANES, 2 * LANES)), _const_spec((1, Q_W)), _const_spec((1, Q_W)),
                  tab_spec, tab_spec, tab_spec],
        out_specs=[row_spec(Q_W), row_spec(Q_W), row_spec(Q_W), row_spec(GROUP_W)],
        out_shape=[jax.ShapeDtypeStruct((rows, Q_W), out_dtype)] * 3 + [jax.ShapeDtypeStruct((rows, GROUP_W), out_dtype)],
        compiler_params=_params("parallel"),
        name="qkv_proj",
    )(x2d, gkv, gq, wkv, wq, ebd, kgain, qgain, *tabs)


def _rope_tables(pos):
    half = ROT_DIM // 2
    inv_freq = jnp.exp(-math.log(ROPE_THETA) * jnp.arange(half, dtype=F32) * (2.0 / ROT_DIM))
    ang = pos.astype(F32)[:, None] * inv_freq[None, :]
    cos, sin = jnp.cos(ang), jnp.sin(ang)
    ones = jnp.ones((pos.shape[0], HEAD_DIM - ROT_DIM), F32)
    zeros8 = jnp.zeros((pos.shape[0], half), F32)
    zeros = jnp.zeros_like(ones)
    c64 = jnp.concatenate([cos, cos, ones], axis=1)
    sa64 = jnp.concatenate([-sin, zeros8, zeros], axis=1)
    sb64 = jnp.concatenate([zeros8, sin, zeros], axis=1)
    tile2 = lambda a: jnp.concatenate([a, a], axis=1)
    return tile2(c64), tile2(sa64), tile2(sb64)


def _attn_kernel(q_ref, kc_ref, kh_ref, vc_ref, vh_ref, o_ref, lse_ref, kbuf, vbuf, *, tq):
    i = pl.program_id(2)
    kbuf[0:SPAN, :] = kh_ref[...]
    kbuf[SPAN:SPAN + tq, :] = kc_ref[...]
    vbuf[0:SPAN, :] = vh_ref[...]
    vbuf[SPAN:SPAN + tq, :] = vc_ref[...]
    qi = lax.broadcasted_iota(jnp.int32, (SPAN, 2 * SPAN), 0)
    kj = lax.broadcasted_iota(jnp.int32, (SPAN, 2 * SPAN), 1)
    band = (kj >= qi) & (kj <= qi + SPAN)
    lane = lax.broadcasted_iota(jnp.int32, (SPAN, LANES), 1)
    lo_half = lane < HEAD_DIM

    def block(jb, carry):
        r0 = pl.multiple_of(jb * SPAN, SPAN)
        kmin = jnp.where((i == 0) & (jb == 0), SPAN, 0)
        valid = band & (kj >= kmin)
        q = q_ref[pl.ds(r0, SPAN), :]
        kb = kbuf[pl.ds(r0, 2 * SPAN), :]
        vb = vbuf[pl.ds(r0, 2 * SPAN), :]
        outs = []
        lses = jnp.zeros((SPAN, LANES), F32)
        for hp in range(HEADS // 2):
            sl = slice(hp * LANES, (hp + 1) * LANES)
            qp, kp, vp = q[:, sl], kb[:, sl], vb[:, sl]
            o_pair = None
            for half in range(2):
                mask = lo_half if half == 0 else jnp.logical_not(lo_half)
                qh = jnp.where(mask, qp, jnp.zeros_like(qp))
                s = lax.dot_general(qh, kp, (((1,), (1,)), ((), ())), preferred_element_type=F32)
                s = jnp.where(valid, s, NEG_INF)
                mx = jnp.max(s, axis=-1, keepdims=True)
                e = jnp.exp(s - mx)
                den = jnp.sum(e, axis=-1, keepdims=True)
                o = jnp.dot(e.astype(BF16), vp, preferred_element_type=F32) / den
                lse = mx + jnp.log(den)
                o_pair = o if half == 0 else jnp.where(lo_half, o_pair, o)
                lses = jnp.where((lane & (HEADS - 1)) == 2 * hp + half, lse, lses)
            outs.append(o_pair)
        o_ref[pl.ds(r0, SPAN), :] = jnp.concatenate(outs, axis=1).astype(o_ref.dtype)
        lse_ref[pl.ds(r0, SPAN), :] = jnp.where(lane < N_GROUPS * HEADS, lses, 0.0)
        return carry

    lax.fori_loop(0, tq // SPAN, block, 0)


def _attn(qd, kd, vd, tq):
    b, d, m, _ = qd.shape
    nb = tq // SPAN
    cur = pl.BlockSpec((None, None, tq, GROUP_W), lambda bi, r, i: (bi, r, i, 0))
    halo = pl.BlockSpec((None, None, SPAN, GROUP_W), lambda bi, r, i: (bi, r, jnp.maximum(i * nb - 1, 0), 0))
    return pl.pallas_call(
        functools.partial(_attn_kernel, tq=tq),
        grid=(b, d, m // tq),
        in_specs=[cur, cur, halo, cur, halo],
        out_specs=[cur, pl.BlockSpec((None, None, tq, LANES), lambda bi, r, i: (bi, r, i, 0))],
        out_shape=[jax.ShapeDtypeStruct((b, d, m, GROUP_W), BF16), jax.ShapeDtypeStruct((b, d, m, LANES), F32)],
        scratch_shapes=[pltpu.VMEM((SPAN + tq, GROUP_W), BF16), pltpu.VMEM((SPAN + tq, GROUP_W), BF16)],
        compiler_params=_params("parallel", "parallel", "parallel"),
        name=f"attn_d{d}",
    )(qd, kd, kd, vd, vd)


def _sample_attn_kernel(cache_ref, kvnew_ref, q_ref, o_ref, lse_ref, *, w, d, tlen):
    rows = tlen * HEADS
    kc = cache_ref[:, 0:GROUP_W].astype(BF16)
    vc = cache_ref[:, GROUP_W:2 * GROUP_W].astype(BF16)
    pad = jnp.zeros((LANES - tlen, GROUP_W), F32)
    kn = jnp.concatenate([kvnew_ref[:, 0:GROUP_W], pad], axis=0).astype(BF16)
    vn = jnp.concatenate([kvnew_ref[:, GROUP_W:2 * GROUP_W], pad], axis=0).astype(BF16)
    sub = lax.broadcasted_iota(jnp.int32, (HEADS, GROUP_W), 0)
    lane = lax.broadcasted_iota(jnp.int32, (HEADS, GROUP_W), 1)
    head_lanes = (lane >> 6) == sub
    q = q_ref[...]
    qrows = jnp.concatenate(
        [jnp.where(head_lanes, jnp.broadcast_to(q[t:t + 1, :], (HEADS, GROUP_W)), 0.0) for t in range(tlen)],
        axis=0).astype(BF16)
    dn = (((1,), (1,)), ((), ()))
    s_c = lax.dot_general(qrows, kc, dn, preferred_element_type=F32)
    s_n = lax.dot_general(qrows, kn, dn, preferred_element_type=F32)
    tq_c = lax.broadcasted_iota(jnp.int32, (rows, w), 0) >> 3
    delta_c = w + tq_c - lax.broadcasted_iota(jnp.int32, (rows, w), 1)
    valid_c = (delta_c <= SPAN * d) & ((delta_c & (d - 1)) == 0)
    tq_n = lax.broadcasted_iota(jnp.int32, (rows, LANES), 0) >> 3
    col_n = lax.broadcasted_iota(jnp.int32, (rows, LANES), 1)
    delta_n = tq_n - col_n
    valid_n = (delta_n >= 0) & ((delta_n & (d - 1)) == 0) & (col_n < tlen)
    s_c = jnp.where(valid_c, s_c, NEG_INF)
    s_n = jnp.where(valid_n, s_n, NEG_INF)
    mx = jnp.maximum(jnp.max(s_c, axis=-1, keepdims=True), jnp.max(s_n, axis=-1, keepdims=True))
    e_c = jnp.exp(s_c - mx)
    e_n = jnp.exp(s_n - mx)
    den = jnp.sum(e_c, axis=-1, keepdims=True) + jnp.sum(e_n, axis=-1, keepdims=True)
    o = (jnp.dot(e_c.astype(BF16), vc, preferred_element_type=F32)
         + jnp.dot(e_n.astype(BF16), vn, preferred_element_type=F32)) / den
    lse = mx + jnp.log(den)
    sub_l = lax.broadcasted_iota(jnp.int32, (HEADS, LANES), 0)
    lane_l = lax.broadcasted_iota(jnp.int32, (HEADS, LANES), 1)
    lse_lanes = ((lane_l & (HEADS - 1)) == sub_l) & (lane_l < N_GROUPS * HEADS)
    o_rows, l_rows = [], []
    for t in range(tlen):
        blk = o[t * HEADS:(t + 1) * HEADS, :]
        o_rows.append(jnp.sum(jnp.where(head_lanes, blk, 0.0), axis=0, keepdims=True))
        lb = jnp.broadcast_to(lse[t * HEADS:(t + 1) * HEADS, :], (HEADS, LANES))
        l_rows.append(jnp.sum(jnp.where(lse_lanes, lb, 0.0), axis=0, keepdims=True))
    o_ref[...] = jnp.concatenate(o_rows, axis=0)
    lse_ref[...] = jnp.concatenate(l_rows, axis=0)


def _sample_attn(cache2d, kvnew, q, d):
    db, w, _ = cache2d.shape
    tlen = q.shape[1]
    return pl.pallas_call(
        functools.partial(_sample_attn_kernel, w=w, d=d, tlen=tlen),
        grid=(db,),
        in_specs=[pl.BlockSpec((None, w, 2 * GROUP_W), lambda b: (b, 0, 0)),
                  pl.BlockSpec((None, tlen, 2 * GROUP_W), lambda b: (b, 0, 0)),
                  pl.BlockSpec((None, tlen, GROUP_W), lambda b: (b, 0, 0))],
        out_specs=[pl.BlockSpec((None, tlen, GROUP_W), lambda b: (b, 0, 0)),
                   pl.BlockSpec((None, tlen, LANES), lambda b: (b, 0, 0))],
        out_shape=[jax.ShapeDtypeStruct((db, tlen, GROUP_W), F32), jax.ShapeDtypeStruct((db, tlen, LANES), F32)],
        compiler_params=_params("parallel"),
        name=f"sample_attn_d{d}",
    )(cache2d, kvnew, q)


def _merge_kernel(o1_ref, o2_ref, o3_ref, l1_ref, l2_ref, l3_ref, zs_ref, x_ref, eexp_ref, w_ref, out_ref):
    a1, a2, a3 = l1_ref[...], l2_ref[...], l3_ref[...]
    top = jnp.maximum(jnp.maximum(a1, a2), a3)
    e1, e2, e3 = jnp.exp(a1 - top), jnp.exp(a2 - top), jnp.exp(a3 - top)
    lane = lax.broadcasted_iota(jnp.int32, a1.shape, 1)
    wcat = jnp.where(lane < HEADS, e1, jnp.where(lane < 2 * HEADS, e2, e3)) / (e1 + e2 + e3)
    hi = wcat.astype(BF16)
    lo = (wcat - hi.astype(F32)).astype(BF16)
    wb = jnp.dot(jnp.concatenate([hi, lo], axis=1), eexp_ref[...], preferred_element_type=F32)
    o = (wb[:, 0:GROUP_W] * o1_ref[...].astype(F32)
         + wb[:, GROUP_W:2 * GROUP_W] * o2_ref[...].astype(F32)
         + wb[:, 2 * GROUP_W:3 * GROUP_W] * o3_ref[...].astype(F32))
    u = (o * zs_ref[...].astype(F32)).astype(BF16)
    out_ref[...] = x_ref[...] + jnp.dot(u, w_ref[...], preferred_element_type=F32)


def _merge(os_, ls, zs, x2d, eexp, wout, tm):
    rows = x2d.shape[0]
    row_spec = lambda w: pl.BlockSpec((tm, w), lambda i: (i, 0))
    return pl.pallas_call(
        _merge_kernel,
        grid=(rows // tm,),
        in_specs=[row_spec(GROUP_W)] * 3 + [row_spec(LANES)] * 3 + [row_spec(GROUP_W), row_spec(D_MODEL),
                  _const_spec((2 * LANES, Q_W)), _const_spec((GROUP_W, D_MODEL))],
        out_specs=row_spec(D_MODEL),
        out_shape=jax.ShapeDtypeStruct((rows, D_MODEL), F32),
        compiler_params=_params("parallel"),
        name="merge_out",
    )(*os_, *ls, zs, x2d, eexp, wout)


def _head_sum_matrix():
    r = jnp.arange(2 * LANES)
    return (r[:, None] // HEAD_DIM == r[None, :] // HEAD_DIM).astype(BF16)


def _expand_matrix():
    r = jnp.arange(2 * LANES) % LANES
    c = jnp.arange(Q_W)
    hit = (r[:, None] < N_GROUPS * HEADS) & (r[:, None] // HEADS == c[None, :] // GROUP_W) \
        & (r[:, None] % HEADS == (c[None, :] % GROUP_W) // HEAD_DIM)
    return hit.astype(BF16)


def _to_residues(a, d):
    b, s, w = a.shape
    return a.reshape(b, s // d, d, w).transpose(0, 2, 1, 3)


def _from_residues(a):
    b, d, m, w = a.shape
    return a.transpose(0, 2, 1, 3).reshape(b * m * d, w)


def kernel(x_prompt, x_sample, cache_conv, cache_kv_w128, cache_kv_w512, cache_kv_w2048, a_norm, a_w_in, a_conv_w, a_conv_b, a_ln_g, a_ln_b, a_w_out, kv_norm, w_kv, k_norm, b_norm, b_w_in, q_norm, b_w_out):
    b, s, _ = x_prompt.shape
    db, tlen, _ = x_sample.shape
    caches = (cache_kv_w128, cache_kv_w512, cache_kv_w2048)
    row = lambda a: a.reshape(1, -1)

    w_in = a_w_in[0].astype(BF16)
    w_out_a = a_w_out[0].astype(BF16)
    g_a = row(a_norm[0])
    cw_pack = a_conv_w[0].reshape(CONV_WIDTH, CH_GROUPS, LANES).astype(BF16)
    cb, lg, lb = row(a_conv_b[0]), row(a_ln_g[0]), row(a_ln_b[0])

    vmaj_p, zs_p = _inproj(x_prompt.reshape(b * s, D_MODEL), g_a, w_in, 256)
    vmaj_p = vmaj_p.reshape(b, s, CH_GROUPS, LANES)
    x1_p = _conv_block(vmaj_p, zs_p.reshape(b, s, CONV_CH), x_prompt, cw_pack, cb, lg, lb, w_out_a, 256)
    new_conv_prompt = vmaj_p[:, s - (CONV_WIDTH - 1):].reshape(1, b, CONV_WIDTH - 1, CONV_CH).astype(F32)

    xs2d = x_sample.reshape(db * tlen, D_MODEL)
    vmaj_s, zs_s = _inproj(xs2d, g_a, w_in, db * tlen)
    v_s = vmaj_s.reshape(db, tlen, CONV_CH).astype(F32)
    full_s = jnp.concatenate([cache_conv[0], v_s], axis=1)
    new_conv_sample = full_s[:, tlen:][None]
    x1_s_t = _conv_sample(full_s.transpose(1, 0, 2), zs_s.reshape(db, tlen, CONV_CH).transpose(1, 0, 2),
                          x_sample.transpose(1, 0, 2), a_conv_w[0], cb, lg, lb, w_out_a)
    x1_s = x1_s_t.transpose(1, 0, 2).reshape(db * tlen, D_MODEL)

    wkv = w_kv.astype(BF16)
    wq = b_w_in[0].astype(BF16)
    w_out_b = b_w_out[0].astype(BF16)
    gkv, gq = row(kv_norm), row(b_norm[0])
    kgain = jnp.tile(k_norm, Q_W // HEAD_DIM).reshape(1, Q_W)
    qgain = jnp.tile(q_norm[0], Q_W // HEAD_DIM).reshape(1, Q_W)
    ebd, eexp = _head_sum_matrix(), _expand_matrix()

    tabs_p = _rope_tables(jnp.arange(s, dtype=jnp.int32))
    q_p, k_p, v_p, zs2_p = _qkv(x1_p.reshape(b * s, D_MODEL), gkv, gq, wkv, wq, ebd, kgain, qgain,
                                tabs_p, s // 256, 256, BF16)
    pos_s = PAST_LEN + jnp.arange(tlen, dtype=jnp.int32)
    tabs_s = tuple(jnp.tile(tb, (db, 1)) for tb in _rope_tables(pos_s))
    q_s, k_s, v_s2, zs2_s = _qkv(x1_s, gkv, gq, wkv, wq, ebd, kgain, qgain, tabs_s, 1, db * tlen, F32)

    q_p, k_p, v_p = (a.reshape(b, s, Q_W) for a in (q_p, k_p, v_p))
    os_p, ls_p, os_s, ls_s, kv_prompt, kv_sample = [], [], [], [], [], []
    for g in range(N_GROUPS):
        d, w = DILATIONS[g], WINDOWS[g]
        gs = slice(g * GROUP_W, (g + 1) * GROUP_W)
        qd, kd, vd = (_to_residues(a[:, :, gs], d) for a in (q_p, k_p, v_p))
        o_d, l_d = _attn(qd, kd, vd, min(512, s // d))
        os_p.append(_from_residues(o_d))
        ls_p.append(_from_residues(l_d))
        rows_w = min(w, s)
        kv_prompt.append(jnp.stack([k_p[:, s - rows_w:, gs], v_p[:, s - rows_w:, gs]], axis=2)
                         .reshape(b, rows_w, 2, HEADS, HEAD_DIM).astype(F32))

        cache = caches[g]
        wlen = cache.shape[1]
        kvnew = jnp.concatenate([k_s[:, gs], v_s2[:, gs]], axis=1).reshape(db, tlen, 2 * GROUP_W)
        cache2d = cache.reshape(db, wlen, 2 * GROUP_W)
        o_g, l_g = _sample_attn(cache2d, kvnew, q_s[:, gs].reshape(db, tlen, GROUP_W), d)
        os_s.append(o_g.reshape(db * tlen, GROUP_W))
        ls_s.append(l_g.reshape(db * tlen, LANES))
        kv_sample.append(jnp.concatenate([cache2d[:, tlen:], kvnew], axis=1).reshape(cache.shape))

    y_p = _merge(os_p, ls_p, zs2_p, x1_p.reshape(b * s, D_MODEL), eexp, w_out_b, 512).reshape(b, s, D_MODEL)
    y_s = _merge(os_s, ls_s, zs2_s, x1_s, eexp, w_out_b, db * tlen).reshape(db, tlen, D_MODEL)

    return (y_p, y_s, new_conv_prompt, new_conv_sample,
            kv_prompt[0], kv_sample[0], kv_prompt[1], kv_sample[1], kv_prompt[2], kv_sample[2])
```

```python
import functools
import math

import jax
import jax.numpy as jnp
from jax import lax
from jax.experimental import pallas as pl
from jax.experimental.pallas import tpu as pltpu

F32 = jnp.float32
BF16 = jnp.bfloat16

D_MODEL = 1024
CONV_CH = 2048
CONV_WIDTH = 31
HEAD_DIM = 64
HEADS = 8
GROUP_W = HEADS * HEAD_DIM
N_GROUPS = 3
Q_W = N_GROUPS * GROUP_W
WINDOWS = (128, 512, 2048)
DILATIONS = (1, 4, 16)
SPAN = 128
ROT_DIM = 16
ROPE_THETA = 500000.0
EPS = 1e-6
ATTN_SCALE = HEAD_DIM ** -0.5
NEG_INF = -1e30
PAST_LEN = 16384

LANES = 128
SUBLANES = 8
CH_GROUPS = CONV_CH // LANES
HALO = 32
VMEM_LIMIT = 56 * 1024 * 1024


def _sigmoid(x):
    return 1.0 / (1.0 + jnp.exp(-x))


def _const_spec(shape):
    return pl.BlockSpec(shape, lambda *_: (0,) * len(shape), pipeline_mode=pl.Buffered(1))


def _params(*sem):
    return pltpu.CompilerParams(dimension_semantics=sem, vmem_limit_bytes=VMEM_LIMIT)


def _inproj_kernel(x_ref, g_ref, w_ref, vmaj_ref, zs_ref, nat_ref, *, tm):
    x = x_ref[...]
    ms = jnp.mean(x * x, axis=-1, keepdims=True)
    h = ((x * lax.rsqrt(ms + EPS)) * g_ref[...]).astype(BF16)
    ch = 2 * LANES
    for j in range(CONV_CH // ch):
        val = jnp.dot(h, w_ref[:, j * ch:(j + 1) * ch], preferred_element_type=F32)
        gate = jnp.dot(h, w_ref[:, CONV_CH + j * ch:CONV_CH + (j + 1) * ch], preferred_element_type=F32)
        z = jnp.dot(h, w_ref[:, 2 * CONV_CH + j * ch:2 * CONV_CH + (j + 1) * ch], preferred_element_type=F32)
        v = val * _sigmoid(gate)
        zs_ref[:, j * ch:(j + 1) * ch] = (z * _sigmoid(z)).astype(BF16)
        for tb in range(tm // SUBLANES):
            for gg in range(ch // LANES):
                r = (tb * CH_GROUPS + j * (ch // LANES) + gg) * SUBLANES
                nat_ref[r:r + SUBLANES, :] = v[tb * SUBLANES:(tb + 1) * SUBLANES, gg * LANES:(gg + 1) * LANES]

    def to_major(tb, carry):
        base = tb * (CH_GROUPS * SUBLANES)
        for i in range(SUBLANES):
            lo = nat_ref[pl.ds(base + i, SUBLANES, stride=SUBLANES), :]
            hi = nat_ref[pl.ds(base + SUBLANES * SUBLANES + i, SUBLANES, stride=SUBLANES), :]
            vmaj_ref[tb * SUBLANES + i] = jnp.concatenate([lo, hi], axis=0).astype(BF16)
        return carry

    lax.fori_loop(0, tm // SUBLANES, to_major, 0)


def _inproj(x2d, gain, w_bf16, tm):
    rows = x2d.shape[0]
    return pl.pallas_call(
        functools.partial(_inproj_kernel, tm=tm),
        grid=(rows // tm,),
        in_specs=[pl.BlockSpec((tm, D_MODEL), lambda i: (i, 0)),
                  _const_spec((1, D_MODEL)),
                  _const_spec((D_MODEL, 3 * CONV_CH))],
        out_specs=[pl.BlockSpec((tm, CH_GROUPS, LANES), lambda i: (i, 0, 0)),
                   pl.BlockSpec((tm, CONV_CH), lambda i: (i, 0))],
        out_shape=[jax.ShapeDtypeStruct((rows, CH_GROUPS, LANES), BF16),
                   jax.ShapeDtypeStruct((rows, CONV_CH), BF16)],
        scratch_shapes=[pltpu.VMEM((tm * CH_GROUPS, LANES), F32)],
        compiler_params=_params("parallel"),
        name="inproj_glu",
    )(x2d, gain, w_bf16)


def _conv_taps(get_row, w_ref):
    accs = [None] * 4
    rows = {}
    for k in range(CONV_WIDTH):
        wk = w_ref[k].astype(F32)
        for u in range(4):
            j = u + k
            if j not in rows:
                rows[j] = get_row(j).astype(F32)
            term = wk * rows[j]
            accs[u] = term if accs[u] is None else accs[u] + term
    return accs


def _store_conv_rows(nat_ref, tb, i0, accs):
    base = tb * (CH_GROUPS * SUBLANES)
    for u, acc in enumerate(accs):
        nat_ref[pl.ds(base + i0 + u, SUBLANES, stride=SUBLANES), :] = acc[0:SUBLANES]
        nat_ref[pl.ds(base + SUBLANES * SUBLANES + i0 + u, SUBLANES, stride=SUBLANES), :] = acc[SUBLANES:]


def _ln_gate_rows(nat_ref, zs_ref, u_ref, cb_ref, lg_ref, lb_ref, nblk):
    def body(t2, carry):
        halves = []
        for half in range(2):
            base = pl.multiple_of((2 * t2 + half) * (CH_GROUPS * SUBLANES), CH_GROUPS * SUBLANES)
            halves.append(jnp.concatenate(
                [nat_ref[pl.ds(base + g * SUBLANES, SUBLANES), :] for g in range(CH_GROUPS)], axis=1))
        c = jnp.concatenate(halves, axis=0) + cb_ref[...]
        mean = jnp.mean(c, axis=-1, keepdims=True)
        xc = c - mean
        var = jnp.mean(xc * xc, axis=-1, keepdims=True)
        y = xc * lax.rsqrt(var + EPS) * lg_ref[...] + lb_ref[...]
        y = y * _sigmoid(y)
        r0 = pl.multiple_of(t2 * 16, 16)
        u_ref[pl.ds(r0, 16), :] = (y * zs_ref[pl.ds(r0, 16), :].astype(F32)).astype(BF16)
        return carry

    lax.fori_loop(0, nblk, body, 0, unroll=2)


def _conv_kernel(vmaj_ref, vhalo_ref, zs_ref, x_ref, w_ref, cb_ref, lg_ref, lb_ref, wout_ref, o_ref,
                 nat_ref, u_ref, vbuf_ref, *, t):
    first = pl.program_id(1) == 0
    halo = vhalo_ref[...]
    vbuf_ref[0:HALO] = jnp.where(first, jnp.zeros_like(halo), halo)
    vbuf_ref[HALO:HALO + t] = vmaj_ref[...]

    def group(gi, carry):
        accs = _conv_taps(lambda j: vbuf_ref[gi * 4 + 2 + j], w_ref)
        _store_conv_rows(nat_ref, gi >> 1, (gi & 1) * 4, accs)
        return carry

    lax.fori_loop(0, t // 4, group, 0)

    _ln_gate_rows(nat_ref, zs_ref, u_ref, cb_ref, lg_ref, lb_ref, t // 16)
    o_ref[...] = x_ref[...] + jnp.dot(u_ref[...], wout_ref[...], preferred_element_type=F32)


def _conv_block(vmaj, zs, x, w_pack, cb, lg, lb, wout, t):
    b, s = x.shape[0], x.shape[1]
    hb = t // HALO
    return pl.pallas_call(
        functools.partial(_conv_kernel, t=t),
        grid=(b, s // t),
        in_specs=[pl.BlockSpec((None, t, CH_GROUPS, LANES), lambda bi, i: (bi, i, 0, 0)),
                  pl.BlockSpec((None, HALO, CH_GROUPS, LANES), lambda bi, i: (bi, jnp.maximum(i * hb - 1, 0), 0, 0)),
                  pl.BlockSpec((None, t, CONV_CH), lambda bi, i: (bi, i, 0)),
                  pl.BlockSpec((None, t, D_MODEL), lambda bi, i: (bi, i, 0)),
                  _const_spec((CONV_WIDTH, CH_GROUPS, LANES)),
                  _const_spec((1, CONV_CH)), _const_spec((1, CONV_CH)), _const_spec((1, CONV_CH)),
                  _const_spec((CONV_CH, D_MODEL))],
        out_specs=pl.BlockSpec((None, t, D_MODEL), lambda bi, i: (bi, i, 0)),
        out_shape=jax.ShapeDtypeStruct((b, s, D_MODEL), F32),
        scratch_shapes=[pltpu.VMEM((t * CH_GROUPS, LANES), F32), pltpu.VMEM((t, CONV_CH), BF16),
                        pltpu.VMEM((HALO + t, CH_GROUPS, LANES), BF16)],
        compiler_params=_params("parallel", "parallel"),
        name="conv_block",
    )(vmaj, vmaj, zs, x, w_pack, cb, lg, lb, wout)


def _conv_sample_kernel(full_ref, zs_ref, x_ref, w_ref, cb_ref, lg_ref, lb_ref, wout_ref, o_ref, *, tlen):
    lc = 4 * LANES
    for t in range(tlen):
        parts = []
        for c in range(CONV_CH // lc):
            acc = None
            for k in range(CONV_WIDTH):
                term = w_ref[k:k + 1, c * lc:(c + 1) * lc] * full_ref[t + k, :, c * lc:(c + 1) * lc]
                acc = term if acc is None else acc + term
            parts.append(acc)
        c_row = jnp.concatenate(parts, axis=1) + cb_ref[...]
        mean = jnp.mean(c_row, axis=-1, keepdims=True)
        xc = c_row - mean
        var = jnp.mean(xc * xc, axis=-1, keepdims=True)
        y = xc * lax.rsqrt(var + EPS) * lg_ref[...] + lb_ref[...]
        y = y * _sigmoid(y)
        u = (y * zs_ref[t].astype(F32)).astype(BF16)
        o_ref[t] = x_ref[t] + jnp.dot(u, wout_ref[...], preferred_element_type=F32)


def _conv_sample(full_t, zs_t, x_t, w, cb, lg, lb, wout):
    tlen, db = x_t.shape[0], x_t.shape[1]
    return pl.pallas_call(
        functools.partial(_conv_sample_kernel, tlen=tlen),
        out_shape=jax.ShapeDtypeStruct((tlen, db, D_MODEL), F32),
        compiler_params=pltpu.CompilerParams(vmem_limit_bytes=VMEM_LIMIT),
        name="conv_sample",
    )(full_t, zs_t, x_t, w, cb, lg, lb, wout)


def _qkv_kernel(x_ref, gkv_ref, gq_ref, wkv_ref, wq_ref, ebd_ref, kgain_ref, qgain_ref, cos_ref, sa_ref, sb_ref,
                q_ref, k_ref, v_ref, zs_ref):
    x = x_ref[...]
    xn = x * lax.rsqrt(jnp.mean(x * x, axis=-1, keepdims=True) + EPS)
    hk = (xn * gkv_ref[...]).astype(BF16)
    hq = (xn * gq_ref[...]).astype(BF16)
    cos, sa, sb = cos_ref[...], sa_ref[...], sb_ref[...]
    ebd = ebd_ref[...]
    cw = 2 * LANES

    def norm_rope(tile, gain):
        ss = jnp.dot((tile * tile).astype(BF16), ebd, preferred_element_type=F32)
        tn = tile * lax.rsqrt(ss * (1.0 / HEAD_DIM) + EPS) * gain
        outs = []
        for hh in range(cw // LANES):
            a = tn[:, hh * LANES:(hh + 1) * LANES]
            outs.append(a * cos + pltpu.roll(a, LANES - ROT_DIM // 2, 1) * sa + pltpu.roll(a, ROT_DIM // 2, 1) * sb)
        return jnp.concatenate(outs, axis=1)

    for c in range(Q_W // cw):
        sl = slice(c * cw, (c + 1) * cw)
        kc = jnp.dot(hk, wkv_ref[:, sl], preferred_element_type=F32)
        k_ref[:, sl] = norm_rope(kc, kgain_ref[:, sl]).astype(k_ref.dtype)
        vc = jnp.dot(hk, wkv_ref[:, Q_W + c * cw:Q_W + (c + 1) * cw], preferred_element_type=F32)
        v_ref[:, sl] = vc.astype(v_ref.dtype)
        qc = jnp.dot(hq, wq_ref[:, sl], preferred_element_type=F32)
        q_ref[:, sl] = (norm_rope(qc, qgain_ref[:, sl]) * ATTN_SCALE).astype(q_ref.dtype)
    for c in range(GROUP_W // cw):
        z = jnp.dot(hq, wq_ref[:, Q_W + c * cw:Q_W + (c + 1) * cw], preferred_element_type=F32)
        zs_ref[:, c * cw:(c + 1) * cw] = (z * _sigmoid(z)).astype(zs_ref.dtype)


def _qkv(x2d, gkv, gq, wkv, wq, ebd, kgain, qgain, tabs, tab_blocks, tm, out_dtype):
    rows = x2d.shape[0]
    tab_spec = pl.BlockSpec((tm, LANES), lambda i: (i % tab_blocks, 0))
    row_spec = lambda w: pl.BlockSpec((tm, w), lambda i: (i, 0))
    return pl.pallas_call(
        _qkv_kernel,
        grid=(rows // tm,),
        in_specs=[row_spec(D_MODEL), _const_spec((1, D_MODEL)), _const_spec((1, D_MODEL)),
                  _const_spec((D_MODEL, 2 * Q_W)), _const_spec((D_MODEL, Q_W + GROUP_W)),
                  _const_spec((2 * LANES, 2 * LANES)), _const_spec((1, Q_W)), _const_spec((1, Q_W)),
                  tab_spec, tab_spec, tab_spec],
        out_specs=[row_spec(Q_W), row_spec(Q_W), row_spec(Q_W), row_spec(GROUP_W)],
        out_shape=[jax.ShapeDtypeStruct((rows, Q_W), out_dtype)] * 3 + [jax.ShapeDtypeStruct((rows, GROUP_W), out_dtype)],
        compiler_params=_params("parallel"),
        name="qkv_proj",
    )(x2d, gkv, gq, wkv, wq, ebd, kgain, qgain, *tabs)


def _rope_tables(pos):
    half = ROT_DIM // 2
    inv_freq = jnp.exp(-math.log(ROPE_THETA) * jnp.arange(half, dtype=F32) * (2.0 / ROT_DIM))
    ang = pos.astype(F32)[:, None] * inv_freq[None, :]
    cos, sin = jnp.cos(ang), jnp.sin(ang)
    ones = jnp.ones((pos.shape[0], HEAD_DIM - ROT_DIM), F32)
    zeros8 = jnp.zeros((pos.shape[0], half), F32)
    zeros = jnp.zeros_like(ones)
    c64 = jnp.concatenate([cos, cos, ones], axis=1)
    sa64 = jnp.concatenate([-sin, zeros8, zeros], axis=1)
    sb64 = jnp.concatenate([zeros8, sin, zeros], axis=1)
    tile2 = lambda a: jnp.concatenate([a, a], axis=1)
    return tile2(c64), tile2(sa64), tile2(sb64)


def _attn_kernel(q_ref, kc_ref, kh_ref, vc_ref, vh_ref, o_ref, lse_ref, kbuf, vbuf, *, tq):
    i = pl.program_id(2)
    kbuf[0:SPAN, :] = kh_ref[...]
    kbuf[SPAN:SPAN + tq, :] = kc_ref[...]
    vbuf[0:SPAN, :] = vh_ref[...]
    vbuf[SPAN:SPAN + tq, :] = vc_ref[...]
    qi = lax.broadcasted_iota(jnp.int32, (SPAN, 2 * SPAN), 0)
    kj = lax.broadcasted_iota(jnp.int32, (SPAN, 2 * SPAN), 1)
    band = (kj >= qi) & (kj <= qi + SPAN)
    lane = lax.broadcasted_iota(jnp.int32, (SPAN, LANES), 1)
    lo_half = lane < HEAD_DIM

    def block(jb, carry):
        r0 = pl.multiple_of(jb * SPAN, SPAN)
        kmin = jnp.where((i == 0) & (jb == 0), SPAN, 0)
        valid = band & (kj >= kmin)
        q = q_ref[pl.ds(r0, SPAN), :]
        kb = kbuf[pl.ds(r0, 2 * SPAN), :]
        vb = vbuf[pl.ds(r0, 2 * SPAN), :]
        outs = []
        lses = jnp.zeros((SPAN, LANES), F32)
        for hp in range(HEADS // 2):
            sl = slice(hp * LANES, (hp + 1) * LANES)
            qp, kp, vp = q[:, sl], kb[:, sl], vb[:, sl]
            o_pair = None
            for half in range(2):
                mask = lo_half if half == 0 else jnp.logical_not(lo_half)
                qh = jnp.where(mask, qp, jnp.zeros_like(qp))
                s = lax.dot_general(qh, kp, (((1,), (1,)), ((), ())), preferred_element_type=F32)
                s = jnp.where(valid, s, NEG_INF)
                mx = jnp.max(s, axis=-1, keepdims=True)
                e = jnp.exp(s - mx)
                den = jnp.sum(e, axis=-1, keepdims=True)
                o = jnp.dot(e.astype(BF16), vp, preferred_element_type=F32) / den
                lse = mx + jnp.log(den)
                o_pair = o if half == 0 else jnp.where(lo_half, o_pair, o)
                lses = jnp.where((lane & (HEADS - 1)) == 2 * hp + half, lse, lses)
            outs.append(o_pair)
        o_ref[pl.ds(r0, SPAN), :] = jnp.concatenate(outs, axis=1).astype(o_ref.dtype)
        lse_ref[pl.ds(r0, SPAN), :] = jnp.where(lane < N_GROUPS * HEADS, lses, 0.0)
        return carry

    lax.fori_loop(0, tq // SPAN, block, 0)


def _attn(qd, kd, vd, tq):
    b, d, m, _ = qd.shape
    nb = tq // SPAN
    cur = pl.BlockSpec((None, None, tq, GROUP_W), lambda bi, r, i: (bi, r, i, 0))
    halo = pl.BlockSpec((None, None, SPAN, GROUP_W), lambda bi, r, i: (bi, r, jnp.maximum(i * nb - 1, 0), 0))
    return pl.pallas_call(
        functools.partial(_attn_kernel, tq=tq),
        grid=(b, d, m // tq),
        in_specs=[cur, cur, halo, cur, halo],
        out_specs=[cur, pl.BlockSpec((None, None, tq, LANES), lambda bi, r, i: (bi, r, i, 0))],
        out_shape=[jax.ShapeDtypeStruct((b, d, m, GROUP_W), BF16), jax.ShapeDtypeStruct((b, d, m, LANES), F32)],
        scratch_shapes=[pltpu.VMEM((SPAN + tq, GROUP_W), BF16), pltpu.VMEM((SPAN + tq, GROUP_W), BF16)],
        compiler_params=_params("parallel", "parallel", "parallel"),
        name=f"attn_d{d}",
    )(qd, kd, kd, vd, vd)


def _sample_attn_kernel(cache_ref, kvnew_ref, q_ref, o_ref, lse_ref, *, w, d, tlen):
    rows = tlen * HEADS
    kc = cache_ref[:, 0:GROUP_W].astype(BF16)
    vc = cache_ref[:, GROUP_W:2 * GROUP_W].astype(BF16)
    pad = jnp.zeros((LANES - tlen, GROUP_W), F32)
    kn = jnp.concatenate([kvnew_ref[:, 0:GROUP_W], pad], axis=0).astype(BF16)
    vn = jnp.concatenate([kvnew_ref[:, GROUP_W:2 * GROUP_W], pad], axis=0).astype(BF16)
    sub = lax.broadcasted_iota(jnp.int32, (HEADS, GROUP_W), 0)
    lane = lax.broadcasted_iota(jnp.int32, (HEADS, GROUP_W), 1)
    head_lanes = (lane >> 6) == sub
    q = q_ref[...]
    qrows = jnp.concatenate(
        [jnp.where(head_lanes, jnp.broadcast_to(q[t:t + 1, :], (HEADS, GROUP_W)), 0.0) for t in range(tlen)],
        axis=0).astype(BF16)
    dn = (((1,), (1,)), ((), ()))
    s_c = lax.dot_general(qrows, kc, dn, preferred_element_type=F32)
    s_n = lax.dot_general(qrows, kn, dn, preferred_element_type=F32)
    tq_c = lax.broadcasted_iota(jnp.int32, (rows, w), 0) >> 3
    delta_c = w + tq_c - lax.broadcasted_iota(jnp.int32, (rows, w), 1)
    valid_c = (delta_c <= SPAN * d) & ((delta_c & (d - 1)) == 0)
    tq_n = lax.broadcasted_iota(jnp.int32, (rows, LANES), 0) >> 3
    col_n = lax.broadcasted_iota(jnp.int32, (rows, LANES), 1)
    delta_n = tq_n - col_n
    valid_n = (delta_n >= 0) & ((delta_n & (d - 1)) == 0) & (col_n < tlen)
    s_c = jnp.where(valid_c, s_c, NEG_INF)
    s_n = jnp.where(valid_n, s_n, NEG_INF)
    mx = jnp.maximum(jnp.max(s_c, axis=-1, keepdims=True), jnp.max(s_n, axis=-1, keepdims=True))
    e_c = jnp.exp(s_c - mx)
    e_n = jnp.exp(s_n - mx)
    den = jnp.sum(e_c, axis=-1, keepdims=True) + jnp.sum(e_n, axis=-1, keepdims=True)
    o = (jnp.dot(e_c.astype(BF16), vc, preferred_element_type=F32)
         + jnp.dot(e_n.astype(BF16), vn, preferred_element_type=F32)) / den
    lse = mx + jnp.log(den)
    sub_l = lax.broadcasted_iota(jnp.int32, (HEADS, LANES), 0)
    lane_l = lax.broadcasted_iota(jnp.int32, (HEADS, LANES), 1)
    lse_lanes = ((lane_l & (HEADS - 1)) == sub_l) & (lane_l < N_GROUPS * HEADS)
    o_rows, l_rows = [], []
    for t in range(tlen):
        blk = o[t * HEADS:(t + 1) * HEADS, :]
        o_rows.append(jnp.sum(jnp.where(head_lanes, blk, 0.0), axis=0, keepdims=True))
        lb = jnp.broadcast_to(lse[t * HEADS:(t + 1) * HEADS, :], (HEADS, LANES))
        l_rows.append(jnp.sum(jnp.where(lse_lanes, lb, 0.0), axis=0, keepdims=True))
    o_ref[...] = jnp.concatenate(o_rows, axis=0)
    lse_ref[...] = jnp.concatenate(l_rows, axis=0)


def _sample_attn(cache2d, kvnew, q, d):
    db, w, _ = cache2d.shape
    tlen = q.shape[1]
    return pl.pallas_call(
        functools.partial(_sample_attn_kernel, w=w, d=d, tlen=tlen),
        grid=(db,),
        in_specs=[pl.BlockSpec((None, w, 2 * GROUP_W), lambda b: (b, 0, 0)),
                  pl.BlockSpec((None, tlen, 2 * GROUP_W), lambda b: (b, 0, 0)),
                  pl.BlockSpec((None, tlen, GROUP_W), lambda b: (b, 0, 0))],
        out_specs=[pl.BlockSpec((None, tlen, GROUP_W), lambda b: (b, 0, 0)),
                   pl.BlockSpec((None, tlen, LANES), lambda b: (b, 0, 0))],
        out_shape=[jax.ShapeDtypeStruct((db, tlen, GROUP_W), F32), jax.ShapeDtypeStruct((db, tlen, LANES), F32)],
        compiler_params=_params("parallel"),
        name=f"sample_attn_d{d}",
    )(cache2d, kvnew, q)


def _merge_kernel(o1_ref, o2_ref, o3_ref, l1_ref, l2_ref, l3_ref, zs_ref, x_ref, eexp_ref, w_ref, out_ref):
    a1, a2, a3 = l1_ref[...], l2_ref[...], l3_ref[...]
    top = jnp.maximum(jnp.maximum(a1, a2), a3)
    e1, e2, e3 = jnp.exp(a1 - top), jnp.exp(a2 - top), jnp.exp(a3 - top)
    lane = lax.broadcasted_iota(jnp.int32, a1.shape, 1)
    wcat = jnp.where(lane < HEADS, e1, jnp.where(lane < 2 * HEADS, e2, e3)) / (e1 + e2 + e3)
    hi = wcat.astype(BF16)
    lo = (wcat - hi.astype(F32)).astype(BF16)
    wb = jnp.dot(jnp.concatenate([hi, lo], axis=1), eexp_ref[...], preferred_element_type=F32)
    o = (wb[:, 0:GROUP_W] * o1_ref[...].astype(F32)
         + wb[:, GROUP_W:2 * GROUP_W] * o2_ref[...].astype(F32)
         + wb[:, 2 * GROUP_W:3 * GROUP_W] * o3_ref[...].astype(F32))
    u = (o * zs_ref[...].astype(F32)).astype(BF16)
    out_ref[...] = x_ref[...] + jnp.dot(u, w_ref[...], preferred_element_type=F32)


def _merge(os_, ls, zs, x2d, eexp, wout, tm):
    rows = x2d.shape[0]
    row_spec = lambda w: pl.BlockSpec((tm, w), lambda i: (i, 0))
    return pl.pallas_call(
        _merge_kernel,
        grid=(rows // tm,),
        in_specs=[row_spec(GROUP_W)] * 3 + [row_spec(LANES)] * 3 + [row_spec(GROUP_W), row_spec(D_MODEL),
                  _const_spec((2 * LANES, Q_W)), _const_spec((GROUP_W, D_MODEL))],
        out_specs=row_spec(D_MODEL),
        out_shape=jax.ShapeDtypeStruct((rows, D_MODEL), F32),
        compiler_params=_params("parallel"),
        name="merge_out",
    )(*os_, *ls, zs, x2d, eexp, wout)


def _head_sum_matrix():
    r = jnp.arange(2 * LANES)
    return (r[:, None] // HEAD_DIM == r[None, :] // HEAD_DIM).astype(BF16)


def _expand_matrix():
    r = jnp.arange(2 * LANES) % LANES
    c = jnp.arange(Q_W)
    hit = (r[:, None] < N_GROUPS * HEADS) & (r[:, None] // HEADS == c[None, :] // GROUP_W) \
        & (r[:, None] % HEADS == (c[None, :] % GROUP_W) // HEAD_DIM)
    return hit.astype(BF16)


def _to_residues(a, d):
    b, s, w = a.shape
    return a.reshape(b, s // d, d, w).transpose(0, 2, 1, 3)


def _from_residues(a):
    b, d, m, w = a.shape
    return a.transpose(0, 2, 1, 3).reshape(b * m * d, w)


def kernel(x_prompt, x_sample, cache_conv, cache_kv_w128, cache_kv_w512, cache_kv_w2048, a_norm, a_w_in, a_conv_w, a_conv_b, a_ln_g, a_ln_b, a_w_out, kv_norm, w_kv, k_norm, b_norm, b_w_in, q_norm, b_w_out):
    b, s, _ = x_prompt.shape
    db, tlen, _ = x_sample.shape
    caches = (cache_kv_w128, cache_kv_w512, cache_kv_w2048)
    row = lambda a: a.reshape(1, -1)

    w_in = a_w_in[0].astype(BF16)
    w_out_a = a_w_out[0].astype(BF16)
    g_a = row(a_norm[0])
    cw_pack = a_conv_w[0].reshape(CONV_WIDTH, CH_GROUPS, LANES).astype(BF16)
    cb, lg, lb = row(a_conv_b[0]), row(a_ln_g[0]), row(a_ln_b[0])

    vmaj_p, zs_p = _inproj(x_prompt.reshape(b * s, D_MODEL), g_a, w_in, 256)
    vmaj_p = vmaj_p.reshape(b, s, CH_GROUPS, LANES)
    x1_p = _conv_block(vmaj_p, zs_p.reshape(b, s, CONV_CH), x_prompt, cw_pack, cb, lg, lb, w_out_a, 256)
    new_conv_prompt = vmaj_p[:, s - (CONV_WIDTH - 1):].reshape(1, b, CONV_WIDTH - 1, CONV_CH).astype(F32)

    xs2d = x_sample.reshape(db * tlen, D_MODEL)
    vmaj_s, zs_s = _inproj(xs2d, g_a, w_in, db * tlen)
    v_s = vmaj_s.reshape(db, tlen, CONV_CH).astype(F32)
    full_s = jnp.concatenate([cache_conv[0], v_s], axis=1)
    new_conv_sample = full_s[:, tlen:][None]
    x1_s_t = _conv_sample(full_s.transpose(1, 0, 2), zs_s.reshape(db, tlen, CONV_CH).transpose(1, 0, 2),
                          x_sample.transpose(1, 0, 2), a_conv_w[0], cb, lg, lb, w_out_a)
    x1_s = x1_s_t.transpose(1, 0, 2).reshape(db * tlen, D_MODEL)

    wkv = w_kv.astype(BF16)
    wq = b_w_in[0].astype(BF16)
    w_out_b = b_w_out[0].astype(BF16)
    gkv, gq = row(kv_norm), row(b_norm[0])
    kgain = jnp.tile(k_norm, Q_W // HEAD_DIM).reshape(1, Q_W)
    qgain = jnp.tile(q_norm[0], Q_W // HEAD_DIM).reshape(1, Q_W)
    ebd, eexp = _head_sum_matrix(), _expand_matrix()

    tabs_p = _rope_tables(jnp.arange(s, dtype=jnp.int32))
    q_p, k_p, v_p, zs2_p = _qkv(x1_p.reshape(b * s, D_MODEL), gkv, gq, wkv, wq, ebd, kgain, qgain,
                                tabs_p, s // 256, 256, BF16)
    pos_s = PAST_LEN + jnp.arange(tlen, dtype=jnp.int32)
    tabs_s = tuple(jnp.tile(tb, (db, 1)) for tb in _rope_tables(pos_s))
    q_s, k_s, v_s2, zs2_s = _qkv(x1_s, gkv, gq, wkv, wq, ebd, kgain, qgain, tabs_s, 1, db * tlen, F32)

    q_p, k_p, v_p = (a.reshape(b, s, Q_W) for a in (q_p, k_p, v_p))
    os_p, ls_p, os_s, ls_s, kv_prompt, kv_sample = [], [], [], [], [], []
    for g in range(N_GROUPS):
        d, w = DILATIONS[g], WINDOWS[g]
        gs = slice(g * GROUP_W, (g + 1) * GROUP_W)
        qd, kd, vd = (_to_residues(a[:, :, gs], d) for a in (q_p, k_p, v_p))
        o_d, l_d = _attn(qd, kd, vd, min(512, s // d))
        os_p.append(_from_residues(o_d))
        ls_p.append(_from_residues(l_d))
        rows_w = min(w, s)
        kv_prompt.append(jnp.stack([k_p[:, s - rows_w:, gs], v_p[:, s - rows_w:, gs]], axis=2)
                         .reshape(b, rows_w, 2, HEADS, HEAD_DIM).astype(F32))

        cache = caches[g]
        wlen = cache.shape[1]
        kvnew = jnp.concatenate([k_s[:, gs], v_s2[:, gs]], axis=1).reshape(db, tlen, 2 * GROUP_W)
        cache2d = cache.reshape(db, wlen, 2 * GROUP_W)
        o_g, l_g = _sample_attn(cache2d, kvnew, q_s[:, gs].reshape(db, tlen, GROUP_W), d)
        os_s.append(o_g.reshape(db * tlen, GROUP_W))
        ls_s.append(l_g.reshape(db * tlen, LANES))
        kv_sample.append(jnp.concatenate([cache2d[:, tlen:], kvnew], axis=1).reshape(cache.shape))

    y_p = _merge(os_p, ls_p, zs2_p, x1_p.reshape(b * s, D_MODEL), eexp, w_out_b, 512).reshape(b, s, D_MODEL)
    y_s = _merge(os_s, ls_s, zs2_s, x1_s, eexp, w_out_b, db * tlen).reshape(db, tlen, D_MODEL)

    return (y_p, y_s, new_conv_prompt, new_conv_sample,
            kv_prompt[0], kv_sample[0], kv_prompt[1], kv_sample[1], kv_prompt[2], kv_sample[2])
```

```python
import functools
import math

import jax
import jax.numpy as jnp
from jax import lax
from jax.experimental import pallas as pl
from jax.experimental.pallas import tpu as pltpu

F32 = jnp.float32
BF16 = jnp.bfloat16

D_MODEL = 1024
CONV_CH = 2048
CONV_WIDTH = 31
HEAD_DIM = 64
HEADS = 8
GROUP_W = HEADS * HEAD_DIM
N_GROUPS = 3
Q_W = N_GROUPS * GROUP_W
WINDOWS = (128, 512, 2048)
DILATIONS = (1, 4, 16)
SPAN = 128
ROT_DIM = 16
ROPE_THETA = 500000.0
EPS = 1e-6
ATTN_SCALE = HEAD_DIM ** -0.5
NEG_INF = -1e30
PAST_LEN = 16384

LANES = 128
SUBLANES = 8
CH_GROUPS = CONV_CH // LANES
HALO = 32
VMEM_LIMIT = 56 * 1024 * 1024


def _sigmoid(x):
    return 1.0 / (1.0 + jnp.exp(-x))


def _const_spec(shape):
    return pl.BlockSpec(shape, lambda *_: (0,) * len(shape), pipeline_mode=pl.Buffered(1))


def _params(*sem):
    return pltpu.CompilerParams(dimension_semantics=sem, vmem_limit_bytes=VMEM_LIMIT)


def _inproj_kernel(x_ref, g_ref, w_ref, vmaj_ref, zs_ref, nat_ref, *, tm):
    x = x_ref[...]
    ms = jnp.mean(x * x, axis=-1, keepdims=True)
    h = ((x * lax.rsqrt(ms + EPS)) * g_ref[...]).astype(BF16)
    ch = 2 * LANES
    for j in range(CONV_CH // ch):
        val = jnp.dot(h, w_ref[:, j * ch:(j + 1) * ch], preferred_element_type=F32)
        gate = jnp.dot(h, w_ref[:, CONV_CH + j * ch:CONV_CH + (j + 1) * ch], preferred_element_type=F32)
        z = jnp.dot(h, w_ref[:, 2 * CONV_CH + j * ch:2 * CONV_CH + (j + 1) * ch], preferred_element_type=F32)
        v = val * _sigmoid(gate)
        zs_ref[:, j * ch:(j + 1) * ch] = (z * _sigmoid(z)).astype(BF16)
        for tb in range(tm // SUBLANES):
            for gg in range(ch // LANES):
                r = (tb * CH_GROUPS + j * (ch // LANES) + gg) * SUBLANES
                nat_ref[r:r + SUBLANES, :] = v[tb * SUBLANES:(tb + 1) * SUBLANES, gg * LANES:(gg + 1) * LANES]

    def to_major(tb, carry):
        base = tb * (CH_GROUPS * SUBLANES)
        for i in range(SUBLANES):
            lo = nat_ref[pl.ds(base + i, SUBLANES, stride=SUBLANES), :]
            hi = nat_ref[pl.ds(base + SUBLANES * SUBLANES + i, SUBLANES, stride=SUBLANES), :]
            vmaj_ref[tb * SUBLANES + i] = jnp.concatenate([lo, hi], axis=0).astype(BF16)
        return carry

    lax.fori_loop(0, tm // SUBLANES, to_major, 0)


def _inproj(x2d, gain, w_bf16, tm):
    rows = x2d.shape[0]
    return pl.pallas_call(
        functools.partial(_inproj_kernel, tm=tm),
        grid=(rows // tm,),
        in_specs=[pl.BlockSpec((tm, D_MODEL), lambda i: (i, 0)),
                  _const_spec((1, D_MODEL)),
                  _const_spec((D_MODEL, 3 * CONV_CH))],
        out_specs=[pl.BlockSpec((tm, CH_GROUPS, LANES), lambda i: (i, 0, 0)),
                   pl.BlockSpec((tm, CONV_CH), lambda i: (i, 0))],
        out_shape=[jax.ShapeDtypeStruct((rows, CH_GROUPS, LANES), BF16),
                   jax.ShapeDtypeStruct((rows, CONV_CH), BF16)],
        scratch_shapes=[pltpu.VMEM((tm * CH_GROUPS, LANES), F32)],
        compiler_params=_params("parallel"),
        name="inproj_glu",
    )(x2d, gain, w_bf16)


def _conv_taps(get_row, w_ref):
    accs = [None] * 4
    rows = {}
    for k in range(CONV_WIDTH):
        wk = w_ref[k].astype(F32)
        for u in range(4):
            j = u + k
            if j not in rows:
                rows[j] = get_row(j).astype(F32)
            term = wk * rows[j]
            accs[u] = term if accs[u] is None else accs[u] + term
    return accs


def _store_conv_rows(nat_ref, tb, i0, accs):
    base = tb * (CH_GROUPS * SUBLANES)
    for u, acc in enumerate(accs):
        nat_ref[pl.ds(base + i0 + u, SUBLANES, stride=SUBLANES), :] = acc[0:SUBLANES]
        nat_ref[pl.ds(base + SUBLANES * SUBLANES + i0 + u, SUBLANES, stride=SUBLANES), :] = acc[SUBLANES:]


def _ln_gate_rows(nat_ref, zs_ref, u_ref, cb_ref, lg_ref, lb_ref, nblk):
    def body(t2, carry):
        halves = []
        for half in range(2):
            base = pl.multiple_of((2 * t2 + half) * (CH_GROUPS * SUBLANES), CH_GROUPS * SUBLANES)
            halves.append(jnp.concatenate(
                [nat_ref[pl.ds(base + g * SUBLANES, SUBLANES), :] for g in range(CH_GROUPS)], axis=1))
        c = jnp.concatenate(halves, axis=0) + cb_ref[...]
        mean = jnp.mean(c, axis=-1, keepdims=True)
        xc = c - mean
        var = jnp.mean(xc * xc, axis=-1, keepdims=True)
        y = xc * lax.rsqrt(var + EPS) * lg_ref[...] + lb_ref[...]
        y = y * _sigmoid(y)
        r0 = pl.multiple_of(t2 * 16, 16)
        u_ref[pl.ds(r0, 16), :] = (y * zs_ref[pl.ds(r0, 16), :].astype(F32)).astype(BF16)
        return carry

    lax.fori_loop(0, nblk, body, 0, unroll=2)


def _conv_kernel(vmaj_ref, vhalo_ref, zs_ref, x_ref, w_ref, cb_ref, lg_ref, lb_ref, wout_ref, o_ref,
                 nat_ref, u_ref, vbuf_ref, *, t):
    first = pl.program_id(1) == 0
    halo = vhalo_ref[...]
    vbuf_ref[0:HALO] = jnp.where(first, jnp.zeros_like(halo), halo)
    vbuf_ref[HALO:HALO + t] = vmaj_ref[...]

    def group(gi, carry):
        accs = _conv_taps(lambda j: vbuf_ref[gi * 4 + 2 + j], w_ref)
        _store_conv_rows(nat_ref, gi >> 1, (gi & 1) * 4, accs)
        return carry

    lax.fori_loop(0, t // 4, group, 0)

    _ln_gate_rows(nat_ref, zs_ref, u_ref, cb_ref, lg_ref, lb_ref, t // 16)
    o_ref[...] = x_ref[...] + jnp.dot(u_ref[...], wout_ref[...], preferred_element_type=F32)


def _conv_block(vmaj, zs, x, w_pack, cb, lg, lb, wout, t):
    b, s = x.shape[0], x.shape[1]
    hb = t // HALO
    return pl.pallas_call(
        functools.partial(_conv_kernel, t=t),
        grid=(b, s // t),
        in_specs=[pl.BlockSpec((None, t, CH_GROUPS, LANES), lambda bi, i: (bi, i, 0, 0)),
                  pl.BlockSpec((None, HALO, CH_GROUPS, LANES), lambda bi, i: (bi, jnp.maximum(i * hb - 1, 0), 0, 0)),
                  pl.BlockSpec((None, t, CONV_CH), lambda bi, i: (bi, i, 0)),
                  pl.BlockSpec((None, t, D_MODEL), lambda bi, i: (bi, i, 0)),
                  _const_spec((CONV_WIDTH, CH_GROUPS, LANES)),
                  _const_spec((1, CONV_CH)), _const_spec((1, CONV_CH)), _const_spec((1, CONV_CH)),
                  _const_spec((CONV_CH, D_MODEL))],
        out_specs=pl.BlockSpec((None, t, D_MODEL), lambda bi, i: (bi, i, 0)),
        out_shape=jax.ShapeDtypeStruct((b, s, D_MODEL), F32),
        scratch_shapes=[pltpu.VMEM((t * CH_GROUPS, LANES), F32), pltpu.VMEM((t, CONV_CH), BF16),
                        pltpu.VMEM((HALO + t, CH_GROUPS, LANES), BF16)],
        compiler_params=_params("parallel", "parallel"),
        name="conv_block",
    )(vmaj, vmaj, zs, x, w_pack, cb, lg, lb, wout)


def _conv_sample_kernel(full_ref, zs_ref, x_ref, w_ref, cb_ref, lg_ref, lb_ref, wout_ref, o_ref, *, tlen):
    lc = 4 * LANES
    for t in range(tlen):
        parts = []
        for c in range(CONV_CH // lc):
            acc = None
            for k in range(CONV_WIDTH):
                term = w_ref[k:k + 1, c * lc:(c + 1) * lc] * full_ref[t + k, :, c * lc:(c + 1) * lc]
                acc = term if acc is None else acc + term
            parts.append(acc)
        c_row = jnp.concatenate(parts, axis=1) + cb_ref[...]
        mean = jnp.mean(c_row, axis=-1, keepdims=True)
        xc = c_row - mean
        var = jnp.mean(xc * xc, axis=-1, keepdims=True)
        y = xc * lax.rsqrt(var + EPS) * lg_ref[...] + lb_ref[...]
        y = y * _sigmoid(y)
        u = (y * zs_ref[t].astype(F32)).astype(BF16)
        o_ref[t] = x_ref[t] + jnp.dot(u, wout_ref[...], preferred_element_type=F32)


def _conv_sample(full_t, zs_t, x_t, w, cb, lg, lb, wout):
    tlen, db = x_t.shape[0], x_t.shape[1]
    return pl.pallas_call(
        functools.partial(_conv_sample_kernel, tlen=tlen),
        out_shape=jax.ShapeDtypeStruct((tlen, db, D_MODEL), F32),
        compiler_params=pltpu.CompilerParams(vmem_limit_bytes=VMEM_LIMIT),
        name="conv_sample",
    )(full_t, zs_t, x_t, w, cb, lg, lb, wout)


def _qkv_kernel(x_ref, gkv_ref, gq_ref, wkv_ref, wq_ref, ebd_ref, kgain_ref, qgain_ref, cos_ref, sa_ref, sb_ref,
                *refs, dils, tm):
    q_refs, k_refs, v_refs, zs_ref, scr_ref = refs[0:3], refs[3:6], refs[6:9], refs[9], refs[10]
    x = x_ref[...]
    xn = x * lax.rsqrt(jnp.mean(x * x, axis=-1, keepdims=True) + EPS)
    hk = (xn * gkv_ref[...]).astype(BF16)
    hq = (xn * gq_ref[...]).astype(BF16)
    cos, sa, sb = cos_ref[...], sa_ref[...], sb_ref[...]
    ebd = ebd_ref[...]
    cw = 2 * LANES
    per_group = GROUP_W // cw

    def norm_rope(tile, gain):
        ss = jnp.dot((tile * tile).astype(BF16), ebd, preferred_element_type=F32)
        tn = tile * lax.rsqrt(ss * (1.0 / HEAD_DIM) + EPS) * gain
        outs = []
        for hh in range(cw // LANES):
            a = tn[:, hh * LANES:(hh + 1) * LANES]
            outs.append(a * cos + pltpu.roll(a, LANES - ROT_DIM // 2, 1) * sa + pltpu.roll(a, ROT_DIM // 2, 1) * sb)
        return jnp.concatenate(outs, axis=1)

    slots = [0]

    def emit(dst_refs, c, val):
        g, half = divmod(c, per_group)
        d, dst = dils[g], dst_refs[g]
        if d == 1:
            dst[0, :, half * cw:(half + 1) * cw] = val.astype(dst.dtype)
            return
        slot = slots[0]
        slots[0] += cw // LANES
        for hh in range(cw // LANES):
            scr_ref[slot + hh] = val[:, hh * LANES:(hh + 1) * LANES]
        for r in range(d):
            for hh in range(cw // LANES):
                piece = scr_ref[slot + hh, pl.ds(r, tm // d, stride=d), :]
                lo = half * cw + hh * LANES
                dst[r, :, lo:lo + LANES] = piece.astype(dst.dtype)

    for c in range(Q_W // cw):
        sl = slice(c * cw, (c + 1) * cw)
        kc = jnp.dot(hk, wkv_ref[:, sl], preferred_element_type=F32)
        emit(k_refs, c, norm_rope(kc, kgain_ref[:, sl]))
        vc = jnp.dot(hk, wkv_ref[:, Q_W + c * cw:Q_W + (c + 1) * cw], preferred_element_type=F32)
        emit(v_refs, c, vc)
        qc = jnp.dot(hq, wq_ref[:, sl], preferred_element_type=F32)
        emit(q_refs, c, norm_rope(qc, qgain_ref[:, sl]) * ATTN_SCALE)
    for c in range(per_group):
        z = jnp.dot(hq, wq_ref[:, Q_W + c * cw:Q_W + (c + 1) * cw], preferred_element_type=F32)
        zs_ref[:, c * cw:(c + 1) * cw] = (z * _sigmoid(z)).astype(zs_ref.dtype)


def _qkv(x2d, batch, gkv, gq, wkv, wq, ebd, kgain, qgain, tabs, dils, tm, out_dtype):
    rows = x2d.shape[0]
    s = rows // batch
    nt = s // tm
    row_spec = lambda w: pl.BlockSpec((tm, w), lambda b, i: (b * nt + i, 0))
    tab_spec = pl.BlockSpec((tm, LANES), lambda b, i: (i, 0))
    grp_specs = [pl.BlockSpec((None, d, tm // d, GROUP_W), lambda b, i: (b, 0, i, 0)) for d in dils]
    grp_shapes = [jax.ShapeDtypeStruct((batch, d, s // d, GROUP_W), out_dtype) for d in dils]
    n_slots = 3 * sum(GROUP_W // LANES for d in dils if d > 1)
    return pl.pallas_call(
        functools.partial(_qkv_kernel, dils=dils, tm=tm),
        grid=(batch, nt),
        in_specs=[row_spec(D_MODEL), _const_spec((1, D_MODEL)), _const_spec((1, D_MODEL)),
                  _const_spec((D_MODEL, 2 * Q_W)), _const_spec((D_MODEL, Q_W + GROUP_W)),
                  _const_spec((2 * LANES, 2 * LANES)), _const_spec((1, Q_W)), _const_spec((1, Q_W)),
                  tab_spec, tab_spec, tab_spec],
        out_specs=grp_specs * 3 + [row_spec(GROUP_W)],
        out_shape=grp_shapes * 3 + [jax.ShapeDtypeStruct((rows, GROUP_W), out_dtype)],
        scratch_shapes=[pltpu.VMEM((max(n_slots, 1), tm, LANES), F32)],
        compiler_params=_params("parallel", "parallel"),
        name="qkv_proj",
    )(x2d, gkv, gq, wkv, wq, ebd, kgain, qgain, *tabs)


def _rope_tables(pos):
    half = ROT_DIM // 2
    inv_freq = jnp.exp(-math.log(ROPE_THETA) * jnp.arange(half, dtype=F32) * (2.0 / ROT_DIM))
    ang = pos.astype(F32)[:, None] * inv_freq[None, :]
    cos, sin = jnp.cos(ang), jnp.sin(ang)
    ones = jnp.ones((pos.shape[0], HEAD_DIM - ROT_DIM), F32)
    zeros8 = jnp.zeros((pos.shape[0], half), F32)
    zeros = jnp.zeros_like(ones)
    c64 = jnp.concatenate([cos, cos, ones], axis=1)
    sa64 = jnp.concatenate([-sin, zeros8, zeros], axis=1)
    sb64 = jnp.concatenate([zeros8, sin, zeros], axis=1)
    tile2 = lambda a: jnp.concatenate([a, a], axis=1)
    return tile2(c64), tile2(sa64), tile2(sb64)


def _attn_kernel(q_ref, kc_ref, kh_ref, vc_ref, vh_ref, o_ref, lse_ref, kbuf, vbuf, *, tq):
    i = pl.program_id(2)
    kbuf[0:SPAN, :] = kh_ref[...]
    kbuf[SPAN:SPAN + tq, :] = kc_ref[...]
    vbuf[0:SPAN, :] = vh_ref[...]
    vbuf[SPAN:SPAN + tq, :] = vc_ref[...]
    qi = lax.broadcasted_iota(jnp.int32, (SPAN, 2 * SPAN), 0)
    kj = lax.broadcasted_iota(jnp.int32, (SPAN, 2 * SPAN), 1)
    band = (kj >= qi) & (kj <= qi + SPAN)
    lane = lax.broadcasted_iota(jnp.int32, (SPAN, LANES), 1)
    lo_half = lane < HEAD_DIM

    def block(jb, carry):
        r0 = pl.multiple_of(jb * SPAN, SPAN)
        kmin = jnp.where((i == 0) & (jb == 0), SPAN, 0)
        valid = band & (kj >= kmin)
        q = q_ref[pl.ds(r0, SPAN), :]
        kb = kbuf[pl.ds(r0, 2 * SPAN), :]
        vb = vbuf[pl.ds(r0, 2 * SPAN), :]
        outs = []
        lses = jnp.zeros((SPAN, LANES), F32)
        for hp in range(HEADS // 2):
            sl = slice(hp * LANES, (hp + 1) * LANES)
            qp, kp, vp = q[:, sl], kb[:, sl], vb[:, sl]
            o_pair = None
            for half in range(2):
                mask = lo_half if half == 0 else jnp.logical_not(lo_half)
                qh = jnp.where(mask, qp, jnp.zeros_like(qp))
                s = lax.dot_general(qh, kp, (((1,), (1,)), ((), ())), preferred_element_type=F32)
                s = jnp.where(valid, s, NEG_INF)
                mx = jnp.max(s, axis=-1, keepdims=True)
                e = jnp.exp(s - mx)
                den = jnp.sum(e, axis=-1, keepdims=True)
                o = jnp.dot(e.astype(BF16), vp, preferred_element_type=F32) / den
                lse = mx + jnp.log(den)
                o_pair = o if half == 0 else jnp.where(lo_half, o_pair, o)
                lses = jnp.where((lane & (HEADS - 1)) == 2 * hp + half, lse, lses)
            outs.append(o_pair)
        o_ref[pl.ds(r0, SPAN), :] = jnp.concatenate(outs, axis=1).astype(o_ref.dtype)
        lse_ref[pl.ds(r0, SPAN), :] = jnp.where(lane < N_GROUPS * HEADS, lses, 0.0)
        return carry

    lax.fori_loop(0, tq // SPAN, block, 0)


def _attn(qd, kd, vd, tq):
    b, d, m, _ = qd.shape
    nb = tq // SPAN
    cur = pl.BlockSpec((None, None, tq, GROUP_W), lambda bi, r, i: (bi, r, i, 0))
    halo = pl.BlockSpec((None, None, SPAN, GROUP_W), lambda bi, r, i: (bi, r, jnp.maximum(i * nb - 1, 0), 0))
    return pl.pallas_call(
        functools.partial(_attn_kernel, tq=tq),
        grid=(b, d, m // tq),
        in_specs=[cur, cur, halo, cur, halo],
        out_specs=[cur, pl.BlockSpec((None, None, tq, LANES), lambda bi, r, i: (bi, r, i, 0))],
        out_shape=[jax.ShapeDtypeStruct((b, d, m, GROUP_W), BF16), jax.ShapeDtypeStruct((b, d, m, LANES), F32)],
        scratch_shapes=[pltpu.VMEM((SPAN + tq, GROUP_W), BF16), pltpu.VMEM((SPAN + tq, GROUP_W), BF16)],
        compiler_params=_params("parallel", "parallel", "parallel"),
        name=f"attn_d{d}",
    )(qd, kd, kd, vd, vd)


def _roll_cache(cache_ref, kvnew_ref, newc_ref, w, tlen):
    ch = 8 * SUBLANES

    def chunk(ci, carry):
        r0 = pl.multiple_of(ci * ch, ch)
        win = cache_ref[pl.ds(r0, ch + SUBLANES), :]
        newc_ref[pl.ds(r0, ch), :] = win[tlen:tlen + ch]
        return carry

    lax.fori_loop(0, w // ch - 1, chunk, 0)
    last = cache_ref[w - ch:w, :]
    newc_ref[w - ch:w - SUBLANES, :] = last[tlen:ch - SUBLANES + tlen]
    newc_ref[w - SUBLANES:w, :] = jnp.concatenate([last[ch - SUBLANES + tlen:ch], kvnew_ref[...]], axis=0)


def _sample_attn_kernel(cache_ref, kvnew_ref, q_ref, newc_ref, o_ref, lse_ref, *, w, d, tlen):
    _roll_cache(cache_ref, kvnew_ref, newc_ref, w, tlen)
    rows = tlen * HEADS
    kc = cache_ref[:, 0:GROUP_W].astype(BF16)
    vc = cache_ref[:, GROUP_W:2 * GROUP_W].astype(BF16)
    pad = jnp.zeros((LANES - tlen, GROUP_W), F32)
    kn = jnp.concatenate([kvnew_ref[:, 0:GROUP_W], pad], axis=0).astype(BF16)
    vn = jnp.concatenate([kvnew_ref[:, GROUP_W:2 * GROUP_W], pad], axis=0).astype(BF16)
    sub = lax.broadcasted_iota(jnp.int32, (HEADS, GROUP_W), 0)
    lane = lax.broadcasted_iota(jnp.int32, (HEADS, GROUP_W), 1)
    head_lanes = (lane >> 6) == sub
    q = q_ref[...]
    qrows = jnp.concatenate(
        [jnp.where(head_lanes, jnp.broadcast_to(q[t:t + 1, :], (HEADS, GROUP_W)), 0.0) for t in range(tlen)],
        axis=0).astype(BF16)
    dn = (((1,), (1,)), ((), ()))
    s_c = lax.dot_general(qrows, kc, dn, preferred_element_type=F32)
    s_n = lax.dot_general(qrows, kn, dn, preferred_element_type=F32)
    tq_c = lax.broadcasted_iota(jnp.int32, (rows, w), 0) >> 3
    delta_c = w + tq_c - lax.broadcasted_iota(jnp.int32, (rows, w), 1)
    valid_c = (delta_c <= SPAN * d) & ((delta_c & (d - 1)) == 0)
    tq_n = lax.broadcasted_iota(jnp.int32, (rows, LANES), 0) >> 3
    col_n = lax.broadcasted_iota(jnp.int32, (rows, LANES), 1)
    delta_n = tq_n - col_n
    valid_n = (delta_n >= 0) & ((delta_n & (d - 1)) == 0) & (col_n < tlen)
    s_c = jnp.where(valid_c, s_c, NEG_INF)
    s_n = jnp.where(valid_n, s_n, NEG_INF)
    mx = jnp.maximum(jnp.max(s_c, axis=-1, keepdims=True), jnp.max(s_n, axis=-1, keepdims=True))
    e_c = jnp.exp(s_c - mx)
    e_n = jnp.exp(s_n - mx)
    den = jnp.sum(e_c, axis=-1, keepdims=True) + jnp.sum(e_n, axis=-1, keepdims=True)
    o = (jnp.dot(e_c.astype(BF16), vc, preferred_element_type=F32)
         + jnp.dot(e_n.astype(BF16), vn, preferred_element_type=F32)) / den
    lse = mx + jnp.log(den)
    sub_l = lax.broadcasted_iota(jnp.int32, (HEADS, LANES), 0)
    lane_l = lax.broadcasted_iota(jnp.int32, (HEADS, LANES), 1)
    lse_lanes = ((lane_l & (HEADS - 1)) == sub_l) & (lane_l < N_GROUPS * HEADS)
    o_rows, l_rows = [], []
    for t in range(tlen):
        blk = o[t * HEADS:(t + 1) * HEADS, :]
        o_rows.append(jnp.sum(jnp.where(head_lanes, blk, 0.0), axis=0, keepdims=True))
        lb = jnp.broadcast_to(lse[t * HEADS:(t + 1) * HEADS, :], (HEADS, LANES))
        l_rows.append(jnp.sum(jnp.where(lse_lanes, lb, 0.0), axis=0, keepdims=True))
    o_ref[...] = jnp.concatenate(o_rows, axis=0)
    lse_ref[...] = jnp.concatenate(l_rows, axis=0)


def _sample_attn(cache2d, kvnew, q, d):
    db, w, _ = cache2d.shape
    tlen = q.shape[1]
    return pl.pallas_call(
        functools.partial(_sample_attn_kernel, w=w, d=d, tlen=tlen),
        grid=(db,),
        in_specs=[pl.BlockSpec((None, w, 2 * GROUP_W), lambda b: (b, 0, 0)),
                  pl.BlockSpec((None, tlen, 2 * GROUP_W), lambda b: (b, 0, 0)),
                  pl.BlockSpec((None, tlen, GROUP_W), lambda b: (b, 0, 0))],
        out_specs=[pl.BlockSpec((None, w, 2 * GROUP_W), lambda b: (b, 0, 0)),
                   pl.BlockSpec((None, tlen, GROUP_W), lambda b: (b, 0, 0)),
                   pl.BlockSpec((None, tlen, LANES), lambda b: (b, 0, 0))],
        out_shape=[jax.ShapeDtypeStruct((db, w, 2 * GROUP_W), F32),
                   jax.ShapeDtypeStruct((db, tlen, GROUP_W), F32), jax.ShapeDtypeStruct((db, tlen, LANES), F32)],
        compiler_params=_params("parallel"),
        name=f"sample_attn_d{d}",
    )(cache2d, kvnew, q)


def _merge_kernel(o1_ref, o2_ref, o3_ref, l1_ref, l2_ref, l3_ref, zs_ref, x_ref, eexp_ref, w_ref, out_ref,
                  onat_ref, lnat_ref, *, dils, tm):
    lane_tiles = GROUP_W // LANES
    slots = [0]

    def natural(o_ref, l_ref, d):
        if d == 1:
            return o_ref[0].astype(F32), l_ref[0]
        slot = slots[0]
        slots[0] += 1
        for r in range(d):
            for lt in range(lane_tiles):
                onat_ref[slot * lane_tiles + lt, pl.ds(r, tm // d, stride=d), :] = (
                    o_ref[r, :, lt * LANES:(lt + 1) * LANES].astype(F32))
            lnat_ref[slot, pl.ds(r, tm // d, stride=d), :] = l_ref[r]
        o = jnp.concatenate([onat_ref[slot * lane_tiles + lt] for lt in range(lane_tiles)], axis=1)
        return o, lnat_ref[slot]

    (o1, a1), (o2, a2), (o3, a3) = (natural(o, l, d) for o, l, d in
                                    zip((o1_ref, o2_ref, o3_ref), (l1_ref, l2_ref, l3_ref), dils))
    top = jnp.maximum(jnp.maximum(a1, a2), a3)
    e1, e2, e3 = jnp.exp(a1 - top), jnp.exp(a2 - top), jnp.exp(a3 - top)
    lane = lax.broadcasted_iota(jnp.int32, a1.shape, 1)
    wcat = jnp.where(lane < HEADS, e1, jnp.where(lane < 2 * HEADS, e2, e3)) / (e1 + e2 + e3)
    hi = wcat.astype(BF16)
    lo = (wcat - hi.astype(F32)).astype(BF16)
    wb = jnp.dot(jnp.concatenate([hi, lo], axis=1), eexp_ref[...], preferred_element_type=F32)
    o = wb[:, 0:GROUP_W] * o1 + wb[:, GROUP_W:2 * GROUP_W] * o2 + wb[:, 2 * GROUP_W:3 * GROUP_W] * o3
    u = (o * zs_ref[...].astype(F32)).astype(BF16)
    out_ref[...] = x_ref[...] + jnp.dot(u, w_ref[...], preferred_element_type=F32)


def _merge(os_, ls, zs, x2d, batch, eexp, wout, tm):
    rows = x2d.shape[0]
    nt = rows // batch // tm
    dils = tuple(o.shape[1] for o in os_)
    row_spec = lambda w: pl.BlockSpec((tm, w), lambda b, i: (b * nt + i, 0))
    grp_spec = lambda d, w: pl.BlockSpec((None, d, tm // d, w), lambda b, i: (b, 0, i, 0))
    n_slots = max(sum(d > 1 for d in dils), 1)
    return pl.pallas_call(
        functools.partial(_merge_kernel, dils=dils, tm=tm),
        grid=(batch, nt),
        in_specs=[grp_spec(d, GROUP_W) for d in dils] + [grp_spec(d, LANES) for d in dils]
                 + [row_spec(GROUP_W), row_spec(D_MODEL), _const_spec((2 * LANES, Q_W)), _const_spec((GROUP_W, D_MODEL))],
        out_specs=row_spec(D_MODEL),
        out_shape=jax.ShapeDtypeStruct((rows, D_MODEL), F32),
        scratch_shapes=[pltpu.VMEM((n_slots * (GROUP_W // LANES), tm, LANES), F32),
                        pltpu.VMEM((n_slots, tm, LANES), F32)],
        compiler_params=_params("parallel", "parallel"),
        name="merge_out",
    )(*os_, *ls, zs, x2d, eexp, wout)


def _head_sum_matrix():
    r = jnp.arange(2 * LANES)
    return (r[:, None] // HEAD_DIM == r[None, :] // HEAD_DIM).astype(BF16)


def _expand_matrix():
    r = jnp.arange(2 * LANES) % LANES
    c = jnp.arange(Q_W)
    hit = (r[:, None] < N_GROUPS * HEADS) & (r[:, None] // HEADS == c[None, :] // GROUP_W) \
        & (r[:, None] % HEADS == (c[None, :] % GROUP_W) // HEAD_DIM)
    return hit.astype(BF16)


def kernel(x_prompt, x_sample, cache_conv, cache_kv_w128, cache_kv_w512, cache_kv_w2048, a_norm, a_w_in, a_conv_w, a_conv_b, a_ln_g, a_ln_b, a_w_out, kv_norm, w_kv, k_norm, b_norm, b_w_in, q_norm, b_w_out):
    b, s, _ = x_prompt.shape
    db, tlen, _ = x_sample.shape
    caches = (cache_kv_w128, cache_kv_w512, cache_kv_w2048)
    row = lambda a: a.reshape(1, -1)

    w_in = a_w_in[0].astype(BF16)
    w_out_a = a_w_out[0].astype(BF16)
    g_a = row(a_norm[0])
    cw_pack = a_conv_w[0].reshape(CONV_WIDTH, CH_GROUPS, LANES).astype(BF16)
    cb, lg, lb = row(a_conv_b[0]), row(a_ln_g[0]), row(a_ln_b[0])

    vmaj_p, zs_p = _inproj(x_prompt.reshape(b * s, D_MODEL), g_a, w_in, 256)
    vmaj_p = vmaj_p.reshape(b, s, CH_GROUPS, LANES)
    x1_p = _conv_block(vmaj_p, zs_p.reshape(b, s, CONV_CH), x_prompt, cw_pack, cb, lg, lb, w_out_a, 256)
    new_conv_prompt = vmaj_p[:, s - (CONV_WIDTH - 1):].reshape(1, b, CONV_WIDTH - 1, CONV_CH).astype(F32)

    xs2d = x_sample.reshape(db * tlen, D_MODEL)
    vmaj_s, zs_s = _inproj(xs2d, g_a, w_in, db * tlen)
    v_s = vmaj_s.reshape(db, tlen, CONV_CH).astype(F32)
    full_s = jnp.concatenate([cache_conv[0], v_s], axis=1)
    new_conv_sample = full_s[:, tlen:][None]
    x1_s_t = _conv_sample(full_s.transpose(1, 0, 2), zs_s.reshape(db, tlen, CONV_CH).transpose(1, 0, 2),
                          x_sample.transpose(1, 0, 2), a_conv_w[0], cb, lg, lb, w_out_a)
    x1_s = x1_s_t.transpose(1, 0, 2).reshape(db * tlen, D_MODEL)

    wkv = w_kv.astype(BF16)
    wq = b_w_in[0].astype(BF16)
    w_out_b = b_w_out[0].astype(BF16)
    gkv, gq = row(kv_norm), row(b_norm[0])
    kgain = jnp.tile(k_norm, Q_W // HEAD_DIM).reshape(1, Q_W)
    qgain = jnp.tile(q_norm[0], Q_W // HEAD_DIM).reshape(1, Q_W)
    ebd, eexp = _head_sum_matrix(), _expand_matrix()

    tabs_p = _rope_tables(jnp.arange(s, dtype=jnp.int32))
    *qkv_p, zs2_p = _qkv(x1_p.reshape(b * s, D_MODEL), b, gkv, gq, wkv, wq, ebd, kgain, qgain,
                         tabs_p, DILATIONS, 256, BF16)
    pos_s = PAST_LEN + jnp.arange(tlen, dtype=jnp.int32)
    tabs_s = tuple(jnp.tile(tb, (db, 1)) for tb in _rope_tables(pos_s))
    rows_s = db * tlen
    *qkv_s, zs2_s = _qkv(x1_s, 1, gkv, gq, wkv, wq, ebd, kgain, qgain, tabs_s, (1,) * N_GROUPS, rows_s, F32)

    os_p, ls_p, os_s, ls_s, kv_prompt, kv_sample = [], [], [], [], [], []
    for g in range(N_GROUPS):
        d, w = DILATIONS[g], WINDOWS[g]
        qd, kd, vd = qkv_p[g], qkv_p[N_GROUPS + g], qkv_p[2 * N_GROUPS + g]
        o_d, l_d = _attn(qd, kd, vd, min(512, s // d))
        os_p.append(o_d)
        ls_p.append(l_d)
        m_rows = min(w, s) // d
        tail = lambda a: a[:, :, s // d - m_rows:].transpose(0, 2, 1, 3).reshape(b, m_rows * d, HEADS, HEAD_DIM)
        kv_prompt.append(jnp.stack([tail(kd), tail(vd)], axis=2).astype(F32))

        cache = caches[g]
        wlen = cache.shape[1]
        q_g, k_g, v_g = (qkv_s[i * N_GROUPS + g].reshape(db, tlen, GROUP_W) for i in range(3))
        kvnew = jnp.concatenate([k_g, v_g], axis=2)
        newc, o_g, l_g = _sample_attn(cache.reshape(db, wlen, 2 * GROUP_W), kvnew, q_g, d)
        os_s.append(o_g.reshape(1, 1, rows_s, GROUP_W))
        ls_s.append(l_g.reshape(1, 1, rows_s, LANES))
        kv_sample.append(newc.reshape(cache.shape))

    y_p = _merge(os_p, ls_p, zs2_p, x1_p.reshape(b * s, D_MODEL), b, eexp, w_out_b, 512).reshape(b, s, D_MODEL)
    y_s = _merge(os_s, ls_s, zs2_s, x1_s, 1, eexp, w_out_b, rows_s).reshape(db, tlen, D_MODEL)

    return (y_p, y_s, new_conv_prompt, new_conv_sample,
            kv_prompt[0], kv_sample[0], kv_prompt[1], kv_sample[1], kv_prompt[2], kv_sample[2])
```

```python
import functools
import math

import jax
import jax.numpy as jnp
from jax import lax
from jax.experimental import pallas as pl
from jax.experimental.pallas import tpu as pltpu

F32 = jnp.float32
BF16 = jnp.bfloat16

D_MODEL = 1024
CONV_CH = 2048
CONV_WIDTH = 31
HEAD_DIM = 64
HEADS = 8
GROUP_W = HEADS * HEAD_DIM
N_GROUPS = 3
Q_W = N_GROUPS * GROUP_W
WINDOWS = (128, 512, 2048)
DILATIONS = (1, 4, 16)
SPAN = 128
ROT_DIM = 16
ROPE_THETA = 500000.0
EPS = 1e-6
ATTN_SCALE = HEAD_DIM ** -0.5
NEG_INF = -1e30
PAST_LEN = 16384

LANES = 128
SUBLANES = 8
CH_GROUPS = CONV_CH // LANES
HALO = 32
VMEM_LIMIT = 56 * 1024 * 1024


def _sigmoid(x):
    return 1.0 / (1.0 + jnp.exp(-x))


def _const_spec(shape):
    return pl.BlockSpec(shape, lambda *_: (0,) * len(shape), pipeline_mode=pl.Buffered(1))


def _params(*sem):
    return pltpu.CompilerParams(dimension_semantics=sem, vmem_limit_bytes=VMEM_LIMIT)


def _inproj_kernel(x_ref, g_ref, w_ref, vmaj_ref, zs_ref, nat_ref, *, tm):
    x = x_ref[...]
    ms = jnp.mean(x * x, axis=-1, keepdims=True)
    h = ((x * lax.rsqrt(ms + EPS)) * g_ref[...]).astype(BF16)
    ch = 2 * LANES
    for j in range(CONV_CH // ch):
        val = jnp.dot(h, w_ref[:, j * ch:(j + 1) * ch], preferred_element_type=F32)
        gate = jnp.dot(h, w_ref[:, CONV_CH + j * ch:CONV_CH + (j + 1) * ch], preferred_element_type=F32)
        z = jnp.dot(h, w_ref[:, 2 * CONV_CH + j * ch:2 * CONV_CH + (j + 1) * ch], preferred_element_type=F32)
        v = val * _sigmoid(gate)
        zs_ref[:, j * ch:(j + 1) * ch] = (z * _sigmoid(z)).astype(BF16)
        for tb in range(tm // SUBLANES):
            for gg in range(ch // LANES):
                r = (tb * CH_GROUPS + j * (ch // LANES) + gg) * SUBLANES
                nat_ref[r:r + SUBLANES, :] = v[tb * SUBLANES:(tb + 1) * SUBLANES, gg * LANES:(gg + 1) * LANES]

    def to_major(tb, carry):
        base = tb * (CH_GROUPS * SUBLANES)
        for i in range(SUBLANES):
            lo = nat_ref[pl.ds(base + i, SUBLANES, stride=SUBLANES), :]
            hi = nat_ref[pl.ds(base + SUBLANES * SUBLANES + i, SUBLANES, stride=SUBLANES), :]
            vmaj_ref[tb * SUBLANES + i] = jnp.concatenate([lo, hi], axis=0).astype(BF16)
        return carry

    lax.fori_loop(0, tm // SUBLANES, to_major, 0)


def _inproj(x2d, gain, w_bf16, tm):
    rows = x2d.shape[0]
    return pl.pallas_call(
        functools.partial(_inproj_kernel, tm=tm),
        grid=(rows // tm,),
        in_specs=[pl.BlockSpec((tm, D_MODEL), lambda i: (i, 0)),
                  _const_spec((1, D_MODEL)),
                  _const_spec((D_MODEL, 3 * CONV_CH))],
        out_specs=[pl.BlockSpec((tm, CH_GROUPS, LANES), lambda i: (i, 0, 0)),
                   pl.BlockSpec((tm, CONV_CH), lambda i: (i, 0))],
        out_shape=[jax.ShapeDtypeStruct((rows, CH_GROUPS, LANES), BF16),
                   jax.ShapeDtypeStruct((rows, CONV_CH), BF16)],
        scratch_shapes=[pltpu.VMEM((tm * CH_GROUPS, LANES), F32)],
        compiler_params=_params("parallel"),
        name="inproj_glu",
    )(x2d, gain, w_bf16)


def _conv_taps(get_row, w_ref):
    accs = [None] * 4
    rows = {}
    for k in range(CONV_WIDTH):
        wk = w_ref[k].astype(F32)
        for u in range(4):
            j = u + k
            if j not in rows:
                rows[j] = get_row(j).astype(F32)
            term = wk * rows[j]
            accs[u] = term if accs[u] is None else accs[u] + term
    return accs


def _store_conv_rows(nat_ref, tb, i0, accs):
    base = tb * (CH_GROUPS * SUBLANES)
    for u, acc in enumerate(accs):
        nat_ref[pl.ds(base + i0 + u, SUBLANES, stride=SUBLANES), :] = acc[0:SUBLANES]
        nat_ref[pl.ds(base + SUBLANES * SUBLANES + i0 + u, SUBLANES, stride=SUBLANES), :] = acc[SUBLANES:]


def _ln_gate_rows(nat_ref, zs_ref, u_ref, cb_ref, lg_ref, lb_ref, nblk):
    def body(t2, carry):
        halves = []
        for half in range(2):
            base = pl.multiple_of((2 * t2 + half) * (CH_GROUPS * SUBLANES), CH_GROUPS * SUBLANES)
            halves.append(jnp.concatenate(
                [nat_ref[pl.ds(base + g * SUBLANES, SUBLANES), :] for g in range(CH_GROUPS)], axis=1))
        c = jnp.concatenate(halves, axis=0) + cb_ref[...]
        mean = jnp.mean(c, axis=-1, keepdims=True)
        xc = c - mean
        var = jnp.mean(xc * xc, axis=-1, keepdims=True)
        y = xc * lax.rsqrt(var + EPS) * lg_ref[...] + lb_ref[...]
        yb = y.astype(BF16)
        r0 = pl.multiple_of(t2 * 16, 16)
        u_ref[pl.ds(r0, 16), :] = yb * _sigmoid(yb) * zs_ref[pl.ds(r0, 16), :]
        return carry

    lax.fori_loop(0, nblk, body, 0, unroll=8)


def _conv_kernel(vmaj_ref, vhalo_ref, zs_ref, x_ref, w_ref, cb_ref, lg_ref, lb_ref, wout_ref, o_ref,
                 nat_ref, u_ref, vbuf_ref, *, t):
    first = pl.program_id(1) == 0
    halo = vhalo_ref[...]
    vbuf_ref[0:HALO] = jnp.where(first, jnp.zeros_like(halo), halo)
    vbuf_ref[HALO:HALO + t] = vmaj_ref[...]

    def group(gi, carry):
        accs = _conv_taps(lambda j: vbuf_ref[gi * 4 + 2 + j], w_ref)
        _store_conv_rows(nat_ref, gi >> 1, (gi & 1) * 4, accs)
        return carry

    lax.fori_loop(0, t // 4, group, 0)

    _ln_gate_rows(nat_ref, zs_ref, u_ref, cb_ref, lg_ref, lb_ref, t // 16)
    o_ref[...] = x_ref[...] + jnp.dot(u_ref[...], wout_ref[...], preferred_element_type=F32)


def _conv_block(vmaj, zs, x, w_pack, cb, lg, lb, wout, t):
    b, s = x.shape[0], x.shape[1]
    hb = t // HALO
    return pl.pallas_call(
        functools.partial(_conv_kernel, t=t),
        grid=(b, s // t),
        in_specs=[pl.BlockSpec((None, t, CH_GROUPS, LANES), lambda bi, i: (bi, i, 0, 0)),
                  pl.BlockSpec((None, HALO, CH_GROUPS, LANES), lambda bi, i: (bi, jnp.maximum(i * hb - 1, 0), 0, 0)),
                  pl.BlockSpec((None, t, CONV_CH), lambda bi, i: (bi, i, 0)),
                  pl.BlockSpec((None, t, D_MODEL), lambda bi, i: (bi, i, 0)),
                  _const_spec((CONV_WIDTH, CH_GROUPS, LANES)),
                  _const_spec((1, CONV_CH)), _const_spec((1, CONV_CH)), _const_spec((1, CONV_CH)),
                  _const_spec((CONV_CH, D_MODEL))],
        out_specs=pl.BlockSpec((None, t, D_MODEL), lambda bi, i: (bi, i, 0)),
        out_shape=jax.ShapeDtypeStruct((b, s, D_MODEL), F32),
        scratch_shapes=[pltpu.VMEM((t * CH_GROUPS, LANES), F32), pltpu.VMEM((t, CONV_CH), BF16),
                        pltpu.VMEM((HALO + t, CH_GROUPS, LANES), BF16)],
        compiler_params=_params("parallel", "parallel"),
        name="conv_block",
    )(vmaj, vmaj, zs, x, w_pack, cb, lg, lb, wout)


def _conv_sample_kernel(full_ref, zs_ref, x_ref, w_ref, cb_ref, lg_ref, lb_ref, wout_ref, o_ref, *, tlen):
    lc = 4 * LANES
    for t in range(tlen):
        parts = []
        for c in range(CONV_CH // lc):
            acc = None
            for k in range(CONV_WIDTH):
                term = w_ref[k:k + 1, c * lc:(c + 1) * lc] * full_ref[t + k, :, c * lc:(c + 1) * lc]
                acc = term if acc is None else acc + term
            parts.append(acc)
        c_row = jnp.concatenate(parts, axis=1) + cb_ref[...]
        mean = jnp.mean(c_row, axis=-1, keepdims=True)
        xc = c_row - mean
        var = jnp.mean(xc * xc, axis=-1, keepdims=True)
        y = xc * lax.rsqrt(var + EPS) * lg_ref[...] + lb_ref[...]
        y = y * _sigmoid(y)
        u = (y * zs_ref[t].astype(F32)).astype(BF16)
        o_ref[t] = x_ref[t] + jnp.dot(u, wout_ref[...], preferred_element_type=F32)


def _conv_sample(full_t, zs_t, x_t, w, cb, lg, lb, wout):
    tlen, db = x_t.shape[0], x_t.shape[1]
    return pl.pallas_call(
        functools.partial(_conv_sample_kernel, tlen=tlen),
        out_shape=jax.ShapeDtypeStruct((tlen, db, D_MODEL), F32),
        compiler_params=pltpu.CompilerParams(vmem_limit_bytes=VMEM_LIMIT),
        name="conv_sample",
    )(full_t, zs_t, x_t, w, cb, lg, lb, wout)


def _qkv_kernel(x_ref, gkv_ref, gq_ref, wkv_ref, wq_ref, ebd_ref, kgain_ref, qgain_ref, cos_ref, sa_ref, sb_ref,
                *refs, dils, tm):
    q_refs, k_refs, v_refs, zs_ref, scr_ref = refs[0:3], refs[3:6], refs[6:9], refs[9], refs[10]
    x = x_ref[...]
    xn = x * lax.rsqrt(jnp.mean(x * x, axis=-1, keepdims=True) + EPS)
    hk = (xn * gkv_ref[...]).astype(BF16)
    hq = (xn * gq_ref[...]).astype(BF16)
    cos, sa, sb = cos_ref[...], sa_ref[...], sb_ref[...]
    ebd = ebd_ref[...]
    cw = 2 * LANES
    per_group = GROUP_W // cw

    def norm_rope(tile, gain):
        ss = jnp.dot((tile * tile).astype(BF16), ebd, preferred_element_type=F32)
        tn = tile * lax.rsqrt(ss * (1.0 / HEAD_DIM) + EPS) * gain
        outs = []
        for hh in range(cw // LANES):
            a = tn[:, hh * LANES:(hh + 1) * LANES]
            outs.append(a * cos + pltpu.roll(a, LANES - ROT_DIM // 2, 1) * sa + pltpu.roll(a, ROT_DIM // 2, 1) * sb)
        return jnp.concatenate(outs, axis=1)

    slots = [0]

    def emit(dst_refs, c, val):
        g, half = divmod(c, per_group)
        d, dst = dils[g], dst_refs[g]
        if d == 1:
            dst[0, :, half * cw:(half + 1) * cw] = val.astype(dst.dtype)
            return
        slot = slots[0]
        slots[0] += cw // LANES
        for hh in range(cw // LANES):
            scr_ref[slot + hh] = val[:, hh * LANES:(hh + 1) * LANES]
        for r in range(d):
            for hh in range(cw // LANES):
                piece = scr_ref[slot + hh, pl.ds(r, tm // d, stride=d), :]
                lo = half * cw + hh * LANES
                dst[r, :, lo:lo + LANES] = piece.astype(dst.dtype)

    for c in range(Q_W // cw):
        sl = slice(c * cw, (c + 1) * cw)
        kc = jnp.dot(hk, wkv_ref[:, sl], preferred_element_type=F32)
        emit(k_refs, c, norm_rope(kc, kgain_ref[:, sl]))
        vc = jnp.dot(hk, wkv_ref[:, Q_W + c * cw:Q_W + (c + 1) * cw], preferred_element_type=F32)
        emit(v_refs, c, vc)
        qc = jnp.dot(hq, wq_ref[:, sl], preferred_element_type=F32)
        emit(q_refs, c, norm_rope(qc, qgain_ref[:, sl]) * ATTN_SCALE)
    for c in range(per_group):
        z = jnp.dot(hq, wq_ref[:, Q_W + c * cw:Q_W + (c + 1) * cw], preferred_element_type=F32)
        zs_ref[:, c * cw:(c + 1) * cw] = (z * _sigmoid(z)).astype(zs_ref.dtype)


def _qkv(x2d, batch, gkv, gq, wkv, wq, ebd, kgain, qgain, tabs, dils, tm, out_dtype):
    rows = x2d.shape[0]
    s = rows // batch
    nt = s // tm
    row_spec = lambda w: pl.BlockSpec((tm, w), lambda b, i: (b * nt + i, 0))
    tab_spec = pl.BlockSpec((tm, LANES), lambda b, i: (i, 0))
    grp_specs = [pl.BlockSpec((None, d, tm // d, GROUP_W), lambda b, i: (b, 0, i, 0)) for d in dils]
    grp_shapes = [jax.ShapeDtypeStruct((batch, d, s // d, GROUP_W), out_dtype) for d in dils]
    n_slots = 3 * sum(GROUP_W // LANES for d in dils if d > 1)
    return pl.pallas_call(
        functools.partial(_qkv_kernel, dils=dils, tm=tm),
        grid=(batch, nt),
        in_specs=[row_spec(D_MODEL), _const_spec((1, D_MODEL)), _const_spec((1, D_MODEL)),
                  _const_spec((D_MODEL, 2 * Q_W)), _const_spec((D_MODEL, Q_W + GROUP_W)),
                  _const_spec((2 * LANES, 2 * LANES)), _const_spec((1, Q_W)), _const_spec((1, Q_W)),
                  tab_spec, tab_spec, tab_spec],
        out_specs=grp_specs * 3 + [row_spec(GROUP_W)],
        out_shape=grp_shapes * 3 + [jax.ShapeDtypeStruct((rows, GROUP_W), out_dtype)],
        scratch_shapes=[pltpu.VMEM((max(n_slots, 1), tm, LANES), F32)],
        compiler_params=_params("parallel", "parallel"),
        name="qkv_proj",
    )(x2d, gkv, gq, wkv, wq, ebd, kgain, qgain, *tabs)


def _rope_tables(pos):
    half = ROT_DIM // 2
    inv_freq = jnp.exp(-math.log(ROPE_THETA) * jnp.arange(half, dtype=F32) * (2.0 / ROT_DIM))
    ang = pos.astype(F32)[:, None] * inv_freq[None, :]
    cos, sin = jnp.cos(ang), jnp.sin(ang)
    ones = jnp.ones((pos.shape[0], HEAD_DIM - ROT_DIM), F32)
    zeros8 = jnp.zeros((pos.shape[0], half), F32)
    zeros = jnp.zeros_like(ones)
    c64 = jnp.concatenate([cos, cos, ones], axis=1)
    sa64 = jnp.concatenate([-sin, zeros8, zeros], axis=1)
    sb64 = jnp.concatenate([zeros8, sin, zeros], axis=1)
    tile2 = lambda a: jnp.concatenate([a, a], axis=1)
    return tile2(c64), tile2(sa64), tile2(sb64)


def _attn_kernel(q_ref, kc_ref, kh_ref, vc_ref, vh_ref, o_ref, lse_ref, kbuf, vbuf, *, tq):
    i = pl.program_id(2)
    kbuf[0:SPAN, :] = kh_ref[...]
    kbuf[SPAN:SPAN + tq, :] = kc_ref[...]
    vbuf[0:SPAN, :] = vh_ref[...]
    vbuf[SPAN:SPAN + tq, :] = vc_ref[...]
    qi = lax.broadcasted_iota(jnp.int32, (SPAN, 2 * SPAN), 0)
    kj = lax.broadcasted_iota(jnp.int32, (SPAN, 2 * SPAN), 1)
    band = (kj >= qi) & (kj <= qi + SPAN)
    lane = lax.broadcasted_iota(jnp.int32, (SPAN, LANES), 1)
    lo_half = lane < HEAD_DIM

    def block(jb, carry):
        r0 = pl.multiple_of(jb * SPAN, SPAN)
        kmin = jnp.where((i == 0) & (jb == 0), SPAN, 0)
        valid = band & (kj >= kmin)
        q = q_ref[pl.ds(r0, SPAN), :]
        kb = kbuf[pl.ds(r0, 2 * SPAN), :]
        vb = vbuf[pl.ds(r0, 2 * SPAN), :]
        outs = []
        lses = jnp.zeros((SPAN, LANES), F32)
        for hp in range(HEADS // 2):
            sl = slice(hp * LANES, (hp + 1) * LANES)
            qp, kp, vp = q[:, sl], kb[:, sl], vb[:, sl]
            o_pair = None
            for half in range(2):
                mask = lo_half if half == 0 else jnp.logical_not(lo_half)
                qh = jnp.where(mask, qp, jnp.zeros_like(qp))
                s = lax.dot_general(qh, kp, (((1,), (1,)), ((), ())), preferred_element_type=F32)
                s = jnp.where(valid, s, NEG_INF)
                mx = jnp.max(s, axis=-1, keepdims=True)
                e = jnp.exp(s - mx)
                den = jnp.sum(e, axis=-1, keepdims=True)
                o = jnp.dot(e.astype(BF16), vp, preferred_element_type=F32) / den
                lse = mx + jnp.log(den)
                o_pair = o if half == 0 else jnp.where(lo_half, o_pair, o)
                lses = jnp.where((lane & (HEADS - 1)) == 2 * hp + half, lse, lses)
            outs.append(o_pair)
        o_ref[pl.ds(r0, SPAN), :] = jnp.concatenate(outs, axis=1).astype(o_ref.dtype)
        lse_ref[pl.ds(r0, SPAN), :] = jnp.where(lane < N_GROUPS * HEADS, lses, 0.0)
        return carry

    lax.fori_loop(0, tq // SPAN, block, 0, unroll=4)


def _attn(qd, kd, vd, tq):
    b, d, m, _ = qd.shape
    nb = tq // SPAN
    cur = pl.BlockSpec((None, None, tq, GROUP_W), lambda bi, r, i: (bi, r, i, 0))
    halo = pl.BlockSpec((None, None, SPAN, GROUP_W), lambda bi, r, i: (bi, r, jnp.maximum(i * nb - 1, 0), 0))
    return pl.pallas_call(
        functools.partial(_attn_kernel, tq=tq),
        grid=(b, d, m // tq),
        in_specs=[cur, cur, halo, cur, halo],
        out_specs=[cur, pl.BlockSpec((None, None, tq, LANES), lambda bi, r, i: (bi, r, i, 0))],
        out_shape=[jax.ShapeDtypeStruct((b, d, m, GROUP_W), BF16), jax.ShapeDtypeStruct((b, d, m, LANES), F32)],
        scratch_shapes=[pltpu.VMEM((SPAN + tq, GROUP_W), BF16), pltpu.VMEM((SPAN + tq, GROUP_W), BF16)],
        compiler_params=_params("parallel", "parallel", "parallel"),
        name=f"attn_d{d}",
    )(qd, kd, kd, vd, vd)


def _sample_attn_kernel(kv_ref, kvnew_ref, q_ref, cache_hbm, newc_hbm, o_ref, lse_ref, sem, *, w, d, tlen, res, n_dma):
    b = pl.program_id(0)
    piece = (w - tlen) // n_dma
    copies = [pltpu.make_async_copy(cache_hbm.at[b, pl.ds(tlen + k * piece, piece)],
                                    newc_hbm.at[b, pl.ds(k * piece, piece)], sem.at[k]) for k in range(n_dma)]
    copies.append(pltpu.make_async_copy(kvnew_ref.at[0], newc_hbm.at[b, pl.ds(w - tlen, tlen)], sem.at[n_dma]))
    for cp in copies:
        cp.start()

    rows = tlen * HEADS
    n = SPAN * res * HEADS
    kmat = kv_ref[:, :, 0:HEADS, :].reshape(n, HEAD_DIM).astype(BF16)
    vmat = kv_ref[:, :, HEADS:2 * HEADS, :].reshape(n, HEAD_DIM).astype(BF16)
    pad = jnp.zeros((LANES - rows, HEAD_DIM), F32)
    kn = jnp.concatenate([kvnew_ref[0, :, 0:HEADS, :].reshape(rows, HEAD_DIM), pad], axis=0).astype(BF16)
    vn = jnp.concatenate([kvnew_ref[0, :, HEADS:2 * HEADS, :].reshape(rows, HEAD_DIM), pad], axis=0).astype(BF16)
    qb = q_ref[...].astype(BF16)
    dn = (((1,), (1,)), ((), ()))
    s_c = lax.dot_general(qb, kmat, dn, preferred_element_type=F32)
    s_n = lax.dot_general(qb, kn, dn, preferred_element_type=F32)
    res_bits = res.bit_length() - 1
    r_c = lax.broadcasted_iota(jnp.int32, (rows, n), 0)
    c_c = lax.broadcasted_iota(jnp.int32, (rows, n), 1)
    pos_c = (c_c >> (3 + res_bits)) * d + ((c_c >> 3) & (res - 1))
    delta_c = w + (r_c >> 3) - pos_c
    valid_c = (((r_c ^ c_c) & (HEADS - 1)) == 0) & (delta_c >= 0) & (delta_c <= SPAN * d) & ((delta_c & (d - 1)) == 0)
    r_n = lax.broadcasted_iota(jnp.int32, (rows, LANES), 0)
    c_n = lax.broadcasted_iota(jnp.int32, (rows, LANES), 1)
    delta_n = (r_n >> 3) - (c_n >> 3)
    valid_n = (((r_n ^ c_n) & (HEADS - 1)) == 0) & (delta_n >= 0) & ((delta_n & (d - 1)) == 0) & (c_n < rows)
    s_c = jnp.where(valid_c, s_c, NEG_INF)
    s_n = jnp.where(valid_n, s_n, NEG_INF)
    mx = jnp.maximum(jnp.max(s_c, axis=-1, keepdims=True), jnp.max(s_n, axis=-1, keepdims=True))
    e_c = jnp.exp(s_c - mx)
    e_n = jnp.exp(s_n - mx)
    den = jnp.sum(e_c, axis=-1, keepdims=True) + jnp.sum(e_n, axis=-1, keepdims=True)
    o_ref[...] = (jnp.dot(e_c.astype(BF16), vmat, preferred_element_type=F32)
                  + jnp.dot(e_n.astype(BF16), vn, preferred_element_type=F32)) / den
    lse = mx + jnp.log(den)
    sub_l = lax.broadcasted_iota(jnp.int32, (HEADS, LANES), 0)
    lane_l = lax.broadcasted_iota(jnp.int32, (HEADS, LANES), 1)
    lse_lanes = ((lane_l & (HEADS - 1)) == sub_l) & (lane_l < N_GROUPS * HEADS)
    l_rows = []
    for t in range(tlen):
        lb = jnp.broadcast_to(lse[t * HEADS:(t + 1) * HEADS, :], (HEADS, LANES))
        l_rows.append(jnp.sum(jnp.where(lse_lanes, lb, 0.0), axis=0, keepdims=True))
    lse_ref[...] = jnp.concatenate(l_rows, axis=0)

    for cp in copies:
        cp.wait()


def _sample_attn(cache, kvnew, q, d):
    db, w = cache.shape[0], cache.shape[1]
    tlen = kvnew.shape[1]
    res = min(d, 4)
    n_dma = 4
    assert w == SPAN * d and (tlen <= res or d == 1) and (w - tlen) % n_dma == 0
    slab = (2 * HEADS, HEAD_DIM)
    any_spec = pl.BlockSpec(memory_space=pl.ANY)
    return pl.pallas_call(
        functools.partial(_sample_attn_kernel, w=w, d=d, tlen=tlen, res=res, n_dma=n_dma),
        grid=(db,),
        in_specs=[pl.BlockSpec((None, SPAN, res) + slab, lambda b: (b, 0, 0, 0, 0)),
                  pl.BlockSpec((1, tlen) + slab, lambda b: (b, 0, 0, 0)),
                  pl.BlockSpec((None, tlen * HEADS, HEAD_DIM), lambda b: (b, 0, 0)),
                  any_spec],
        out_specs=[any_spec,
                   pl.BlockSpec((None, tlen * HEADS, HEAD_DIM), lambda b: (b, 0, 0)),
                   pl.BlockSpec((None, tlen, LANES), lambda b: (b, 0, 0))],
        out_shape=[jax.ShapeDtypeStruct(cache.shape, F32),
                   jax.ShapeDtypeStruct((db, tlen * HEADS, HEAD_DIM), F32),
                   jax.ShapeDtypeStruct((db, tlen, LANES), F32)],
        scratch_shapes=[pltpu.SemaphoreType.DMA((n_dma + 1,))],
        compiler_params=_params("arbitrary"),
        name=f"sample_attn_d{d}",
    )(cache.reshape(db, SPAN, d, 2 * HEADS, HEAD_DIM), kvnew, q, cache)


def _merge_kernel(o1_ref, o2_ref, o3_ref, l1_ref, l2_ref, l3_ref, zs_ref, x_ref, eexp_ref, w_ref, out_ref,
                  onat_ref, lnat_ref, *, dils, tm):
    lane_tiles = GROUP_W // LANES
    slots = [0]

    def natural(o_ref, l_ref, d):
        if d == 1:
            return o_ref[0].astype(F32), l_ref[0]
        slot = slots[0]
        slots[0] += 1
        for r in range(d):
            for lt in range(lane_tiles):
                onat_ref[slot * lane_tiles + lt, pl.ds(r, tm // d, stride=d), :] = (
                    o_ref[r, :, lt * LANES:(lt + 1) * LANES].astype(F32))
            lnat_ref[slot, pl.ds(r, tm // d, stride=d), :] = l_ref[r]
        o = jnp.concatenate([onat_ref[slot * lane_tiles + lt] for lt in range(lane_tiles)], axis=1)
        return o, lnat_ref[slot]

    (o1, a1), (o2, a2), (o3, a3) = (natural(o, l, d) for o, l, d in
                                    zip((o1_ref, o2_ref, o3_ref), (l1_ref, l2_ref, l3_ref), dils))
    top = jnp.maximum(jnp.maximum(a1, a2), a3)
    e1, e2, e3 = jnp.exp(a1 - top), jnp.exp(a2 - top), jnp.exp(a3 - top)
    lane = lax.broadcasted_iota(jnp.int32, a1.shape, 1)
    wcat = jnp.where(lane < HEADS, e1, jnp.where(lane < 2 * HEADS, e2, e3)) / (e1 + e2 + e3)
    hi = wcat.astype(BF16)
    lo = (wcat - hi.astype(F32)).astype(BF16)
    wb = jnp.dot(jnp.concatenate([hi, lo], axis=1), eexp_ref[...], preferred_element_type=F32)
    o = wb[:, 0:GROUP_W] * o1 + wb[:, GROUP_W:2 * GROUP_W] * o2 + wb[:, 2 * GROUP_W:3 * GROUP_W] * o3
    u = (o * zs_ref[...].astype(F32)).astype(BF16)
    out_ref[...] = x_ref[...] + jnp.dot(u, w_ref[...], preferred_element_type=F32)


def _merge(os_, ls, zs, x2d, batch, eexp, wout, tm):
    rows = x2d.shape[0]
    nt = rows // batch // tm
    dils = tuple(o.shape[1] for o in os_)
    row_spec = lambda w: pl.BlockSpec((tm, w), lambda b, i: (b * nt + i, 0))
    grp_spec = lambda d, w: pl.BlockSpec((None, d, tm // d, w), lambda b, i: (b, 0, i, 0))
    n_slots = max(sum(d > 1 for d in dils), 1)
    return pl.pallas_call(
        functools.partial(_merge_kernel, dils=dils, tm=tm),
        grid=(batch, nt),
        in_specs=[grp_spec(d, GROUP_W) for d in dils] + [grp_spec(d, LANES) for d in dils]
                 + [row_spec(GROUP_W), row_spec(D_MODEL), _const_spec((2 * LANES, Q_W)), _const_spec((GROUP_W, D_MODEL))],
        out_specs=row_spec(D_MODEL),
        out_shape=jax.ShapeDtypeStruct((rows, D_MODEL), F32),
        scratch_shapes=[pltpu.VMEM((n_slots * (GROUP_W // LANES), tm, LANES), F32),
                        pltpu.VMEM((n_slots, tm, LANES), F32)],
        compiler_params=_params("parallel", "parallel"),
        name="merge_out",
    )(*os_, *ls, zs, x2d, eexp, wout)


def _head_sum_matrix():
    r = jnp.arange(2 * LANES)
    return (r[:, None] // HEAD_DIM == r[None, :] // HEAD_DIM).astype(BF16)


def _expand_matrix():
    r = jnp.arange(2 * LANES) % LANES
    c = jnp.arange(Q_W)
    hit = (r[:, None] < N_GROUPS * HEADS) & (r[:, None] // HEADS == c[None, :] // GROUP_W) \
        & (r[:, None] % HEADS == (c[None, :] % GROUP_W) // HEAD_DIM)
    return hit.astype(BF16)


def kernel(x_prompt, x_sample, cache_conv, cache_kv_w128, cache_kv_w512, cache_kv_w2048, a_norm, a_w_in, a_conv_w, a_conv_b, a_ln_g, a_ln_b, a_w_out, kv_norm, w_kv, k_norm, b_norm, b_w_in, q_norm, b_w_out):
    b, s, _ = x_prompt.shape
    db, tlen, _ = x_sample.shape
    caches = (cache_kv_w128, cache_kv_w512, cache_kv_w2048)
    row = lambda a: a.reshape(1, -1)

    w_in = a_w_in[0].astype(BF16)
    w_out_a = a_w_out[0].astype(BF16)
    g_a = row(a_norm[0])
    cw_pack = a_conv_w[0].reshape(CONV_WIDTH, CH_GROUPS, LANES).astype(BF16)
    cb, lg, lb = row(a_conv_b[0]), row(a_ln_g[0]), row(a_ln_b[0])

    vmaj_p, zs_p = _inproj(x_prompt.reshape(b * s, D_MODEL), g_a, w_in, 256)
    vmaj_p = vmaj_p.reshape(b, s, CH_GROUPS, LANES)
    x1_p = _conv_block(vmaj_p, zs_p.reshape(b, s, CONV_CH), x_prompt, cw_pack, cb, lg, lb, w_out_a, 256)
    new_conv_prompt = vmaj_p[:, s - (CONV_WIDTH - 1):].reshape(1, b, CONV_WIDTH - 1, CONV_CH).astype(F32)

    xs2d = x_sample.reshape(db * tlen, D_MODEL)
    vmaj_s, zs_s = _inproj(xs2d, g_a, w_in, db * tlen)
    v_s = vmaj_s.reshape(db, tlen, CONV_CH).astype(F32)
    full_s = jnp.concatenate([cache_conv[0], v_s], axis=1)
    new_conv_sample = full_s[:, tlen:][None]
    x1_s_t = _conv_sample(full_s.transpose(1, 0, 2), zs_s.reshape(db, tlen, CONV_CH).transpose(1, 0, 2),
                          x_sample.transpose(1, 0, 2), a_conv_w[0], cb, lg, lb, w_out_a)
    x1_s = x1_s_t.transpose(1, 0, 2).reshape(db * tlen, D_MODEL)

    wkv = w_kv.astype(BF16)
    wq = b_w_in[0].astype(BF16)
    w_out_b = b_w_out[0].astype(BF16)
    gkv, gq = row(kv_norm), row(b_norm[0])
    kgain = jnp.tile(k_norm, Q_W // HEAD_DIM).reshape(1, Q_W)
    qgain = jnp.tile(q_norm[0], Q_W // HEAD_DIM).reshape(1, Q_W)
    ebd, eexp = _head_sum_matrix(), _expand_matrix()

    tabs_p = _rope_tables(jnp.arange(s, dtype=jnp.int32))
    *qkv_p, zs2_p = _qkv(x1_p.reshape(b * s, D_MODEL), b, gkv, gq, wkv, wq, ebd, kgain, qgain,
                         tabs_p, DILATIONS, 512, BF16)
    pos_s = PAST_LEN + jnp.arange(tlen, dtype=jnp.int32)
    tabs_s = tuple(jnp.tile(tb, (db, 1)) for tb in _rope_tables(pos_s))
    rows_s = db * tlen
    *qkv_s, zs2_s = _qkv(x1_s, 1, gkv, gq, wkv, wq, ebd, kgain, qgain, tabs_s, (1,) * N_GROUPS, rows_s, F32)

    os_p, ls_p, os_s, ls_s, kv_prompt, kv_sample = [], [], [], [], [], []
    for g in range(N_GROUPS):
        d, w = DILATIONS[g], WINDOWS[g]
        qd, kd, vd = qkv_p[g], qkv_p[N_GROUPS + g], qkv_p[2 * N_GROUPS + g]
        o_d, l_d = _attn(qd, kd, vd, min(512, s // d))
        os_p.append(o_d)
        ls_p.append(l_d)
        m_rows = min(w, s) // d
        tail = lambda a: a[:, :, s // d - m_rows:].transpose(0, 2, 1, 3).reshape(b, m_rows * d, HEADS, HEAD_DIM)
        kv_prompt.append(jnp.stack([tail(kd), tail(vd)], axis=2).astype(F32))

        cache = caches[g]
        wlen = cache.shape[1]
        heads = lambda i: qkv_s[i * N_GROUPS + g].reshape(db, tlen, HEADS, HEAD_DIM)
        kvnew = jnp.concatenate([heads(1), heads(2)], axis=2)
        newc, o_g, l_g = _sample_attn(cache.reshape(db, wlen, 2 * HEADS, HEAD_DIM), kvnew,
                                      heads(0).reshape(db, tlen * HEADS, HEAD_DIM), d)
        os_s.append(o_g.reshape(1, 1, rows_s, GROUP_W))
        ls_s.append(l_g.reshape(1, 1, rows_s, LANES))
        kv_sample.append(newc.reshape(cache.shape))

    y_p = _merge(os_p, ls_p, zs2_p, x1_p.reshape(b * s, D_MODEL), b, eexp, w_out_b, 512).reshape(b, s, D_MODEL)
    y_s = _merge(os_s, ls_s, zs2_s, x1_s, 1, eexp, w_out_b, rows_s).reshape(db, tlen, D_MODEL)

    return (y_p, y_s, new_conv_prompt, new_conv_sample,
            kv_prompt[0], kv_sample[0], kv_prompt[1], kv_sample[1], kv_prompt[2], kv_sample[2])
```
